```python
import jax, jax.numpy as jnp
from jax import lax
import numpy as np

D_MODEL = 1024
BATCH = 4
SEQ = 4096
DEPTH = 4
DEC_BATCH = 32
DEC_SEQ = 1
PAST_LEN = 8192
PAGE_SIZE = 128

N_AB = (DEPTH + 1) // 2
N_C = DEPTH // 2
CONV_GROUPS = 8
D_CONV = D_MODEL // 2
CONV_W = 3
N_HEADS_B = 8
HEAD_DIM_B = 64
D_ATT = N_HEADS_B * HEAD_DIM_B
Q_BLOCK = 128
FORGET_BIAS_INIT = 3.0
D_IN_AB = 3 * D_CONV + 3 * D_ATT + N_HEADS_B
CHUNK = 128
SG_GROUPS = 8
D_SG = D_MODEL
SG_GROUP_DIM = D_SG // SG_GROUPS
PEER_HEADS = 8
PEER_NKEYS = 128
PEER_N = PEER_NKEYS * PEER_NKEYS
PEER_DK = 256
PEER_TOPK = 16
PEER_BLOCK = 256
DN_ALPHA = (2.0 * DEPTH) ** 0.25
DN_BETA = (8.0 * DEPTH) ** -0.25
LN_EPS = 1e-5

kernel_name = "hybrid_conv_fox_gmlp_peer_decode_step"


def _layer_norm(x, g, b):
    xf = x.astype(jnp.float32)
    mu = jnp.mean(xf, axis=-1, keepdims=True)
    var = jnp.mean(jnp.square(xf - mu), axis=-1, keepdims=True)
    return ((xf - mu) * lax.rsqrt(var + LN_EPS) * g + b).astype(x.dtype)


def _short_conv(z, prev, w):
    zp = jnp.concatenate([prev.astype(z.dtype), z], axis=1)
    t = z.shape[1]
    y = sum(w[j] * zp[:, j:j + t] for j in range(CONV_W))
    return y, zp[:, -(CONV_W - 1):]


def _fox_prompt(q, k, v, logf):
    n, t, h, d = q.shape
    nb = t // Q_BLOCK
    cum = jnp.cumsum(logf, axis=1)
    qb = q.reshape(n, nb, Q_BLOCK, h, d).transpose(1, 0, 2, 3, 4)
    cb = cum.reshape(n, nb, Q_BLOCK, h).transpose(1, 0, 2, 3)
    starts = jnp.arange(nb) * Q_BLOCK
    key_pos = jnp.arange(t)
    cum_k = cum.transpose(0, 2, 1)[:, :, None, :]
    scale = HEAD_DIM_B ** -0.5

    def block(args):
        qi, ci, s0 = args
        s = jnp.einsum('nqhd,nkhd->nhqk', qi, k, preferred_element_type=jnp.float32) * scale
        s = s + ci.transpose(0, 2, 1)[..., None] - cum_k
        qpos = s0 + jnp.arange(Q_BLOCK)
        s = jnp.where(key_pos[None, :] <= qpos[:, None], s, -jnp.inf)
        p = jax.nn.softmax(s, axis=-1).astype(v.dtype)
        return jnp.einsum('nhqk,nkhd->nqhd', p, v)

    o = lax.map(block, (qb, cb, starts))
    return o.transpose(1, 0, 2, 3, 4).reshape(n, t, h, d)


def _fox_sample(q, k, v, logf, pk, pv, plogf):
    n, t, h, d = q.shape
    p_len = pk.shape[1]
    keys = jnp.concatenate([pk.astype(k.dtype), k], axis=1)
    vals = jnp.concatenate([pv.astype(v.dtype), v], axis=1)
    cum = jnp.cumsum(jnp.concatenate([plogf.astype(jnp.float32), logf], axis=1), axis=1)
    s = jnp.einsum('nqhd,nkhd->nhqk', q, keys, preferred_element_type=jnp.float32) * (HEAD_DIM_B ** -0.5)
    s = s + cum[:, p_len:].transpose(0, 2, 1)[..., None] - cum.transpose(0, 2, 1)[:, :, None, :]
    mask = jnp.arange(p_len + t)[None, :] <= (p_len + jnp.arange(t))[:, None]
    s = jnp.where(mask, s, -jnp.inf)
    p = jax.nn.softmax(s, axis=-1).astype(v.dtype)
    return jnp.einsum('nhqk,nkhd->nqhd', p, vals)


def _ab_mixer(x, w_in, b_f, conv_w, w_o, conv_prev, past):
    n, t, _ = x.shape
    cuts = np.cumsum([D_CONV, D_CONV, D_CONV, D_ATT, D_ATT, D_ATT]).tolist()
    gb, gc, hc, q, k, v, fl = jnp.split(x @ w_in, cuts, axis=-1)
    z = gc * hc
    if conv_prev is None:
        conv_prev = jnp.zeros((n, CONV_W - 1, D_CONV), z.dtype)
    yc, conv_new = _short_conv(z, conv_prev, conv_w)
    a_out = gb * yc
    q = q.reshape(n, t, N_HEADS_B, HEAD_DIM_B)
    k = k.reshape(n, t, N_HEADS_B, HEAD_DIM_B)
    v = v.reshape(n, t, N_HEADS_B, HEAD_DIM_B)
    logf = jax.nn.log_sigmoid((fl + b_f).astype(jnp.float32))
    if past is None:
        o = _fox_prompt(q, k, v, logf)
    else:
        o = _fox_sample(q, k, v, logf, *past)
    out = jnp.concatenate([a_out, o.reshape(n, t, D_ATT)], axis=-1) @ w_o
    return out, k, v, logf, conv_new


def _chunk_spatial(v, w_s, b_s):
    n, t, c = v.shape
    nch = -(-t // CHUNK)
    tp = nch * CHUNK
    vp = jnp.pad(v, ((0, 0), (0, tp - t), (0, 0))).reshape(n, nch, CHUNK, SG_GROUPS, SG_GROUP_DIM)
    w = w_s * jnp.tril(jnp.ones((CHUNK, CHUNK), w_s.dtype))
    y = jnp.einsum('gij,nkjgc->nkigc', w, vp) + b_s.T[None, None, :, :, None]
    return y.reshape(n, tp, c)[:, :t]


def _chunk_mixer(x, w_in, ln_g, ln_b, w_s, b_s, w_o):
    u, v = jnp.split(jax.nn.gelu(x @ w_in), 2, axis=-1)
    v = _layer_norm(v, ln_g, ln_b)
    return (u * _chunk_spatial(v, w_s, b_s)) @ w_o, v


def _peer(x, wq, keys, u_tab, v_tab):
    shp = x.shape
    xf = x.reshape(-1, D_MODEL)
    n = xf.shape[0]
    q = (xf @ wq).reshape(n, PEER_HEADS, 2, PEER_DK // 2)
    s = jnp.einsum('nhpd,hpkd->nhpk', q, keys, preferred_element_type=jnp.float32)
    top_s, top_i = lax.top_k(s, PEER_TOPK)
    cand_s = top_s[:, :, 0, :, None] + top_s[:, :, 1, None, :]
    cand_i = top_i[:, :, 0, :, None] * PEER_NKEYS + top_i[:, :, 1, None, :]
    best_s, pos = lax.top_k(cand_s.reshape(n, PEER_HEADS, PEER_TOPK * PEER_TOPK), PEER_TOPK)
    experts = jnp.take_along_axis(cand_i.reshape(n, PEER_HEADS, PEER_TOPK * PEER_TOPK), pos, axis=-1)
    gate = jax.nn.softmax(best_s, axis=-1).astype(x.dtype)
    experts = experts.reshape(n, PEER_HEADS * PEER_TOPK)
    gate = gate.reshape(n, PEER_HEADS * PEER_TOPK)
    blk = min(PEER_BLOCK, n)
    nb = -(-n // blk)
    pad = nb * blk - n
    xb = jnp.pad(xf, ((0, pad), (0, 0))).reshape(nb, blk, D_MODEL)
    eb = jnp.pad(experts, ((0, pad), (0, 0))).reshape(nb, blk, -1)
    gb = jnp.pad(gate, ((0, pad), (0, 0))).reshape(nb, blk, -1)

    def one(args):
        xi, ei, gi = args
        hi = jax.nn.gelu(jnp.einsum('bed,bd->be', jnp.take(u_tab, ei, axis=0), xi))
        return jnp.einsum('be,bed->bd', gi * hi, jnp.take(v_tab, ei, axis=0))

    y = lax.map(one, (xb, eb, gb))
    return y.reshape(nb * blk, D_MODEL)[:n].reshape(shp)


def setup_inputs(seed: int = 0) -> dict:
    key = jax.random.key(seed)
    ks = jax.random.split(key, 32)
    nrm = lambda k, shp: jax.random.normal(k, shp, jnp.float32)
    n_pages = PAST_LEN // PAGE_SIZE
    n_used = DEC_BATCH * n_pages
    n_phys = n_used + max(1, n_used // 4)
    page_table = jax.random.permutation(ks[0], n_phys)[:n_used].reshape(DEC_BATCH, n_pages).astype(jnp.int32)
    x_prompt = nrm(ks[1], (BATCH, SEQ, D_MODEL))
    x_sample = nrm(ks[2], (DEC_BATCH, DEC_SEQ, D_MODEL))
    cache_k = nrm(ks[3], (n_phys, N_AB, PAGE_SIZE, N_HEADS_B, HEAD_DIM_B))
    cache_v = DN_BETA * nrm(ks[4], (n_phys, N_AB, PAGE_SIZE, N_HEADS_B, HEAD_DIM_B))
    cache_logf = jax.nn.log_sigmoid(FORGET_BIAS_INIT + nrm(ks[5], (n_phys, N_AB, PAGE_SIZE, N_HEADS_B)))
    state_conv = DN_BETA * nrm(ks[6], (DEC_BATCH, N_AB, CONV_W - 1, D_CONV))
    col_scale = np.concatenate([
        np.full(D_CONV, 1.0), np.full(D_CONV, 1.0), np.full(D_CONV, DN_BETA),
        np.full(D_ATT, 1.0), np.full(D_ATT, 1.0), np.full(D_ATT, DN_BETA),
        np.full(N_HEADS_B, 1.0)]) * D_MODEL ** -0.5
    w_in_ab = nrm(ks[7], (N_AB, D_MODEL, D_IN_AB)) * jnp.asarray(col_scale, jnp.float32)
    b_f = FORGET_BIAS_INIT + 0.1 * nrm(ks[8], (N_AB, N_HEADS_B))
    conv_w = nrm(ks[9], (N_AB, CONV_W, D_CONV)) * CONV_W ** -0.5
    w_o_ab = nrm(ks[10], (N_AB, D_CONV + D_ATT, D_MODEL)) * (D_CONV + D_ATT) ** -0.5 * DN_BETA
    w_in_c = nrm(ks[11], (N_C, D_MODEL, 2 * D_SG)) * D_MODEL ** -0.5
    sg_ln_g = 1.0 + 0.02 * nrm(ks[12], (N_C, D_SG))
    sg_ln_b = 0.02 * nrm(ks[13], (N_C, D_SG))
    w_s = nrm(ks[14], (N_C, SG_GROUPS, CHUNK, CHUNK)) * 0.5 * CHUNK ** -0.5
    b_s = 1.0 + 0.02 * nrm(ks[15], (N_C, SG_GROUPS, CHUNK))
    w_o_c = nrm(ks[16], (N_C, D_SG, D_MODEL)) * D_SG ** -0.5 * DN_BETA
    ln1_g = 1.0 + 0.02 * nrm(ks[17], (DEPTH, D_MODEL))
    ln1_b = 0.02 * nrm(ks[18], (DEPTH, D_MODEL))
    ln2_g = 1.0 + 0.02 * nrm(ks[19], (DEPTH, D_MODEL))
    ln2_b = 0.02 * nrm(ks[20], (DEPTH, D_MODEL))
    peer_wq = nrm(ks[21], (DEPTH, D_MODEL, PEER_HEADS * PEER_DK)) * D_MODEL ** -0.5
    peer_keys = nrm(ks[22], (DEPTH, PEER_HEADS, 2, PEER_NKEYS, PEER_DK // 2)) * (PEER_DK // 2) ** -0.5
    peer_u = nrm(ks[23], (DEPTH, PEER_N, D_MODEL)) * D_MODEL ** -0.5
    peer_v = nrm(ks[24], (DEPTH, PEER_N, D_MODEL)) * DN_BETA * PEER_HEADS ** -0.5
    return {"x_prompt": x_prompt, "x_sample": x_sample, "cache_k": cache_k, "cache_v": cache_v,
            "cache_logf": cache_logf, "state_conv": state_conv, "page_table": page_table,
            "w_in_ab": w_in_ab, "b_f": b_f, "conv_w": conv_w, "w_o_ab": w_o_ab,
            "w_in_c": w_in_c, "sg_ln_g": sg_ln_g, "sg_ln_b": sg_ln_b, "w_s": w_s, "b_s": b_s,
            "w_o_c": w_o_c, "ln1_g": ln1_g, "ln1_b": ln1_b, "ln2_g": ln2_g, "ln2_b": ln2_b,
            "peer_wq": peer_wq, "peer_keys": peer_keys, "peer_u": peer_u, "peer_v": peer_v}


def reference(x_prompt, x_sample, cache_k, cache_v, cache_logf, state_conv, page_table,
              w_in_ab, b_f, conv_w, w_o_ab, w_in_c, sg_ln_g, sg_ln_b, w_s, b_s, w_o_c,
              ln1_g, ln1_b, ln2_g, ln2_b, peer_wq, peer_keys, peer_u, peer_v):
    n_dec = x_sample.shape[0]
    p_len = page_table.shape[1] * PAGE_SIZE
    xp, xs = x_prompt, x_sample
    kp_l, vp_l, fp_l, cp_l = [], [], [], []
    ks_l, vs_l, fs_l, cs_l = [], [], [], []
    sv_l = []
    for l in range(DEPTH):
        i = l // 2
        if l % 2 == 0:
            mp, kp, vp, fp, cp = _ab_mixer(xp, w_in_ab[i], b_f[i], conv_w[i], w_o_ab[i], None, None)
            past = (cache_k[page_table, i].reshape(n_dec, p_len, N_HEADS_B, HEAD_DIM_B),
                    cache_v[page_table, i].reshape(n_dec, p_len, N_HEADS_B, HEAD_DIM_B),
                    cache_logf[page_table, i].reshape(n_dec, p_len, N_HEADS_B))
            ms, ks, vs, fs, cs = _ab_mixer(xs, w_in_ab[i], b_f[i], conv_w[i], w_o_ab[i], state_conv[:, i], past)
            kp_l.append(kp); vp_l.append(vp); fp_l.append(fp); cp_l.append(cp)
            ks_l.append(ks); vs_l.append(vs); fs_l.append(fs); cs_l.append(cs)
        else:
            mp, _ = _chunk_mixer(xp, w_in_c[i], sg_ln_g[i], sg_ln_b[i], w_s[i], b_s[i], w_o_c[i])
            ms, sv = _chunk_mixer(xs, w_in_c[i], sg_ln_g[i], sg_ln_b[i], w_s[i], b_s[i], w_o_c[i])
            sv_l.append(sv)
        xp = _layer_norm(DN_ALPHA * xp + mp, ln1_g[l], ln1_b[l])
        xs = _layer_norm(DN_ALPHA * xs + ms, ln1_g[l], ln1_b[l])
        xp = _layer_norm(DN_ALPHA * xp + _peer(xp, peer_wq[l], peer_keys[l], peer_u[l], peer_v[l]), ln2_g[l], ln2_b[l])
        xs = _layer_norm(DN_ALPHA * xs + _peer(xs, peer_wq[l], peer_keys[l], peer_u[l], peer_v[l]), ln2_g[l], ln2_b[l])
    new_k_prompt = jnp.stack(kp_l, axis=1)
    new_v_prompt = jnp.stack(vp_l, axis=1)
    new_logf_prompt = jnp.stack(fp_l, axis=1)
    new_k_sample = jnp.stack(ks_l, axis=1)
    new_v_sample = jnp.stack(vs_l, axis=1)
    new_logf_sample = jnp.stack(fs_l, axis=1)
    new_conv_prompt = jnp.stack(cp_l, axis=1)
    new_conv_sample = jnp.stack(cs_l, axis=1)
    new_sgv_sample = jnp.stack(sv_l, axis=1)
    return (xp, xs, new_k_prompt, new_v_prompt, new_logf_prompt, new_k_sample, new_v_sample,
            new_logf_sample, new_conv_prompt, new_conv_sample, new_sgv_sample)
```

```python
import functools
import math

import jax
import jax.numpy as jnp
from jax import lax
from jax.experimental import pallas as pl
from jax.experimental.pallas import tpu as pltpu

LANES = 128
SUBLANES = 8
VMEM_LIMIT = 48 * 1024 * 1024

LN_EPS = 1e-5
HEAD_DIM = 64
N_HEADS = 8
PAGE = 128
CHUNK = 128
SG_GROUPS = 8
PEER_HEADS = 8
PEER_NKEYS = 128
PEER_TOPK = 16
PEER_SLOTS = PEER_HEADS * PEER_TOPK
GATHER_GROUP = 8
NEG_INF = float("-inf")


def _params(*sem, **kw):
    return pltpu.CompilerParams(dimension_semantics=sem, vmem_limit_bytes=VMEM_LIMIT, **kw)


def _gelu(x):
    return 0.5 * x * (1.0 + jnp.tanh(math.sqrt(2.0 / math.pi) * (x + 0.044715 * (x * x * x))))


def _layer_norm(x, g, b):
    mu = jnp.mean(x, axis=-1, keepdims=True)
    xc = x - mu
    var = jnp.mean(xc * xc, axis=-1, keepdims=True)
    return xc * lax.rsqrt(var + LN_EPS) * g + b


def _log_sigmoid(x):
    return -(jnp.maximum(-x, 0.0) + jnp.log1p(jnp.exp(-jnp.abs(x))))


def _bdot(a, b):
    return jnp.dot(a.astype(jnp.bfloat16), b.astype(jnp.bfloat16), preferred_element_type=jnp.float32)


def _bdot_nt(a, b):
    return lax.dot_general(a.astype(jnp.bfloat16), b.astype(jnp.bfloat16), (((1,), (1,)), ((), ())),
                           preferred_element_type=jnp.float32)


def _row_block(n, pref):
    return pref if n % pref == 0 else n


def _mm_kernel(x_ref, w_ref, o_ref):
    o_ref[...] = _bdot(x_ref[...], w_ref[...])


def _matmul(x, w, tn):
    n, k = x.shape
    m = w.shape[1]
    tm = _row_block(n, 512)
    return pl.pallas_call(
        _mm_kernel,
        grid=(m // tn, n // tm),
        in_specs=[pl.BlockSpec((tm, k), lambda j, i: (i, 0)),
                  pl.BlockSpec((k, tn), lambda j, i: (0, j))],
        out_specs=pl.BlockSpec((tm, tn), lambda j, i: (i, j)),
        out_shape=jax.ShapeDtypeStruct((n, m), jnp.float32),
        compiler_params=_params("arbitrary", "arbitrary"),
        name="matmul",
    )(x, w)


def _mm_res_ln_kernel(*refs, n_in, alpha):
    xs, ws = refs[:n_in], refs[n_in:2 * n_in]
    res_ref, g_ref, b_ref, o_ref = refs[2 * n_in:]
    y = alpha * res_ref[...]
    for x_ref, w_ref in zip(xs, ws):
        y = y + _bdot(x_ref[...], w_ref[...])
    o_ref[...] = _layer_norm(y, g_ref[...], b_ref[...])


def _matmul_res_ln(xs, ws, res, g, b, alpha):
    n, d = res.shape
    tm = _row_block(n, 512)
    row = lambda i: (i, 0)
    fixed = lambda i: (0, 0)
    in_specs = ([pl.BlockSpec((tm, x.shape[1]), row) for x in xs]
                + [pl.BlockSpec(w.shape, fixed) for w in ws]
                + [pl.BlockSpec((tm, d), row), pl.BlockSpec((1, d), fixed), pl.BlockSpec((1, d), fixed)])
    return pl.pallas_call(
        functools.partial(_mm_res_ln_kernel, n_in=len(xs), alpha=alpha),
        grid=(n // tm,),
        in_specs=in_specs,
        out_specs=pl.BlockSpec((tm, d), row),
        out_shape=jax.ShapeDtypeStruct((n, d), jnp.float32),
        compiler_params=_params("arbitrary"),
        name="matmul_res_ln",
    )(*xs, *ws, res, g.reshape(1, d), b.reshape(1, d))


def _conv_prompt_kernel(gb_ref, gc_ref, hc_ref, gcp_ref, hcp_ref, fl_ref, bf_ref, cw_ref,
                        a_ref, lf_ref, cn_ref, zs_ref, *, tm, blocks_per_seq):
    i = pl.program_id(0)
    z = gc_ref[...] * hc_ref[...]
    zprev = gcp_ref[...] * hcp_ref[...]
    zprev = jnp.where(i % blocks_per_seq == 0, jnp.zeros_like(zprev), zprev)
    zs_ref[0:SUBLANES, :] = zprev
    zs_ref[SUBLANES:, :] = z
    z1 = zs_ref[pl.ds(SUBLANES - 1, tm), :]
    z2 = zs_ref[pl.ds(SUBLANES - 2, tm), :]
    cw = cw_ref[...]
    y = cw[0:1, :] * z2 + cw[1:2, :] * z1 + cw[2:3, :] * z
    a_ref[...] = gb_ref[...] * y
    lf_ref[...] = _log_sigmoid(fl_ref[:, 0:N_HEADS] + bf_ref[...])
    cn_ref[0] = zs_ref[pl.ds(tm + SUBLANES - 2, 2), :]


def _conv_prompt(xw, b_f, conv_w, n_seq, d_conv):
    n = xw.shape[0]
    t = n // n_seq
    tm = 512
    bps = t // tm
    fl_blk = (6 * d_conv) // LANES
    col = lambda c: (lambda i: (i, c))
    prev = lambda c: (lambda i: (jnp.maximum(i * (tm // SUBLANES) - 1, 0), c))
    return pl.pallas_call(
        functools.partial(_conv_prompt_kernel, tm=tm, blocks_per_seq=bps),
        grid=(n // tm,),
        in_specs=[pl.BlockSpec((tm, d_conv), col(0)), pl.BlockSpec((tm, d_conv), col(1)),
                  pl.BlockSpec((tm, d_conv), col(2)),
                  pl.BlockSpec((SUBLANES, d_conv), prev(1)), pl.BlockSpec((SUBLANES, d_conv), prev(2)),
                  pl.BlockSpec((tm, LANES), col(fl_blk)),
                  pl.BlockSpec((1, N_HEADS), lambda i: (0, 0)),
                  pl.BlockSpec(conv_w.shape, lambda i: (0, 0))],
        out_specs=[pl.BlockSpec((tm, d_conv), lambda i: (i, 0)),
                   pl.BlockSpec((tm, N_HEADS), lambda i: (i, 0)),
                   pl.BlockSpec((1, 2, d_conv), lambda i: (i // bps, 0, 0))],
        out_shape=[jax.ShapeDtypeStruct((n, d_conv), jnp.float32),
                   jax.ShapeDtypeStruct((n, N_HEADS), jnp.float32),
                   jax.ShapeDtypeStruct((n_seq, 2, d_conv), jnp.float32)],
        scratch_shapes=[pltpu.VMEM((tm + SUBLANES, d_conv), jnp.float32)],
        compiler_params=_params("arbitrary"),
        name="conv_prompt",
    )(xw, xw, xw, xw, xw, xw, b_f.reshape(1, N_HEADS), conv_w)


def _conv_sample_kernel(gb_ref, gc_ref, hc_ref, fl_ref, p0_ref, p1_ref, bf_ref, cw_ref, a_ref, lf_ref, z_ref):
    z = gc_ref[...] * hc_ref[...]
    cw = cw_ref[...]
    y = cw[0:1, :] * p0_ref[...] + cw[1:2, :] * p1_ref[...] + cw[2:3, :] * z
    a_ref[...] = gb_ref[...] * y
    lf_ref[...] = _log_sigmoid(fl_ref[:, 0:N_HEADS] + bf_ref[...])
    z_ref[...] = z


def _conv_sample(xw, prev0, prev1, b_f, conv_w, d_conv):
    n = xw.shape[0]
    fl_blk = (6 * d_conv) // LANES
    full = lambda shp: pl.BlockSpec(shp, lambda i: (0,) * len(shp))
    return pl.pallas_call(
        _conv_sample_kernel,
        grid=(1,),
        in_specs=[pl.BlockSpec((n, d_conv), lambda i: (0, 0)), pl.BlockSpec((n, d_conv), lambda i: (0, 1)),
                  pl.BlockSpec((n, d_conv), lambda i: (0, 2)), pl.BlockSpec((n, LANES), lambda i: (0, fl_blk)),
                  full((n, d_conv)), full((n, d_conv)), full((1, N_HEADS)), full(conv_w.shape)],
        out_specs=[full((n, d_conv)), full((n, N_HEADS)), full((n, d_conv))],
        out_shape=[jax.ShapeDtypeStruct((n, d_conv), jnp.float32),
                   jax.ShapeDtypeStruct((n, N_HEADS), jnp.float32),
                   jax.ShapeDtypeStruct((n, d_conv), jnp.float32)],
        compiler_params=_params("arbitrary"),
        name="conv_sample",
    )(xw, xw, xw, xw, prev0, prev1, b_f.reshape(1, N_HEADS), conv_w)


def _lane_prefix_sum(x):
    lane = lax.broadcasted_iota(jnp.int32, x.shape, 1)
    d = 1
    while d < LANES:
        x = x + jnp.where(lane >= d, pltpu.roll(x, d, 1), 0.0)
        d *= 2
    return x


def _lane_suffix_sum(x):
    lane = lax.broadcasted_iota(jnp.int32, x.shape, 1)
    d = 1
    while d < LANES:
        x = x + jnp.where(lane < LANES - d, pltpu.roll(x, LANES - d, 1), 0.0)
        d *= 2
    return x


def _cumsum_kernel(x_ref, o_ref, *, t):
    carry = jnp.zeros((N_HEADS, 1), jnp.float32)
    for c in range(t // LANES):
        sl = slice(c * LANES, (c + 1) * LANES)
        s = _lane_prefix_sum(x_ref[0, :, sl]) + carry
        o_ref[0, :, sl] = s
        carry = s[:, LANES - 1:LANES]


def _cumsum_time(lf_t):
    n_seq, h, t = lf_t.shape
    return pl.pallas_call(
        functools.partial(_cumsum_kernel, t=t),
        grid=(n_seq,),
        in_specs=[pl.BlockSpec((1, h, t), lambda b: (b, 0, 0))],
        out_specs=pl.BlockSpec((1, h, t), lambda b: (b, 0, 0)),
        out_shape=jax.ShapeDtypeStruct(lf_t.shape, jnp.float32),
        compiler_params=_params("arbitrary"),
        name="cumsum_time",
    )(lf_t)


def _fox_prompt_kernel(q_ref, k_ref, v_ref, cq_ref, ck_ref, o_ref, m_ref, l_ref, acc_ref, *, tq, scale):
    qb, kb = pl.program_id(1), pl.program_id(2)

    @pl.when(kb == 0)
    def _():
        m_ref[...] = jnp.full(m_ref.shape, NEG_INF, jnp.float32)
        l_ref[...] = jnp.zeros(l_ref.shape, jnp.float32)
        acc_ref[...] = jnp.zeros(acc_ref.shape, jnp.float32)

    @pl.when(kb <= qb)
    def _():
        row = qb * tq + lax.broadcasted_iota(jnp.int32, (tq, tq), 0)
        colp = kb * tq + lax.broadcasted_iota(jnp.int32, (tq, tq), 1)
        causal = colp <= row
        lane = lax.broadcasted_iota(jnp.int32, (1, LANES), 1)
        for h in range(N_HEADS):
            pair = slice((h // 2) * LANES, (h // 2 + 1) * LANES)
            own = (lane // HEAD_DIM) == (h % 2)
            qh = jnp.where(own, q_ref[:, pair], 0.0)
            s = _bdot_nt(qh, k_ref[:, pair]) * scale
            s = s + cq_ref[:, h:h + 1] - ck_ref[0, h:h + 1, :]
            s = jnp.where(causal, s, NEG_INF)
            m_old = m_ref[h]
            m_new = jnp.maximum(m_old, jnp.max(s, axis=1, keepdims=True))
            p = jnp.exp(s - m_new)
            a = jnp.exp(m_old - m_new)
            l_ref[h] = a * l_ref[h] + jnp.sum(p, axis=1, keepdims=True)
            acc_ref[h] = a * acc_ref[h] + _bdot(p, v_ref[:, pair])
            m_ref[h] = m_new

    @pl.when(kb == qb)
    def _():
        lane = lax.broadcasted_iota(jnp.int32, (1, LANES), 1)
        for hp in range(N_HEADS // 2):
            o0 = acc_ref[2 * hp] / l_ref[2 * hp]
            o1 = acc_ref[2 * hp + 1] / l_ref[2 * hp + 1]
            o_ref[:, hp * LANES:(hp + 1) * LANES] = jnp.where(lane < HEAD_DIM, o0, o1)


def _fox_prompt(xw, cum, cum_t, n_seq, d_conv):
    n = xw.shape[0]
    t = n // n_seq
    d_att = N_HEADS * HEAD_DIM
    tq = 512
    nb = t // tq
    qc, kc, vc = (3 * d_conv) // d_att, (3 * d_conv) // d_att + 1, (3 * d_conv) // d_att + 2
    return pl.pallas_call(
        functools.partial(_fox_prompt_kernel, tq=tq, scale=HEAD_DIM ** -0.5),
        grid=(n_seq, nb, nb),
        in_specs=[pl.BlockSpec((tq, d_att), lambda b, i, j: (b * nb + i, qc)),
                  pl.BlockSpec((tq, d_att), lambda b, i, j: (b * nb + jnp.minimum(i, j), kc)),
                  pl.BlockSpec((tq, d_att), lambda b, i, j: (b * nb + jnp.minimum(i, j), vc)),
                  pl.BlockSpec((tq, N_HEADS), lambda b, i, j: (b * nb + i, 0)),
                  pl.BlockSpec((1, N_HEADS, tq), lambda b, i, j: (b, 0, jnp.minimum(i, j)))],
        out_specs=pl.BlockSpec((tq, d_att), lambda b, i, j: (b * nb + i, 0)),
        out_shape=jax.ShapeDtypeStruct((n, d_att), jnp.float32),
        scratch_shapes=[pltpu.VMEM((N_HEADS, tq, 1), jnp.float32), pltpu.VMEM((N_HEADS, tq, 1), jnp.float32),
                        pltpu.VMEM((N_HEADS, tq, LANES), jnp.float32)],
        compiler_params=_params("arbitrary", "arbitrary", "arbitrary"),
        name="fox_prompt",
    )(xw, xw, xw, cum, cum_t)


def _fox_sample_kernel(pt_ref, q_ref, kn_ref, vn_ref, lfn_ref, ck_ref, cv_ref, clf_ref, o_ref,
                       qm_ref, m_ref, l_ref, acc_ref, d_ref, *, scale):
    j = pl.program_id(1)
    d_att = N_HEADS * HEAD_DIM
    head_of_lane = lax.broadcasted_iota(jnp.int32, (N_HEADS, d_att), 1) // HEAD_DIM
    head_of_row = lax.broadcasted_iota(jnp.int32, (N_HEADS, d_att), 0)
    own = head_of_lane == head_of_row

    @pl.when(j == 0)
    def _():
        qm = jnp.where(own, jnp.broadcast_to(q_ref[0], (N_HEADS, d_att)), 0.0)
        qm_ref[...] = qm
        m_ref[...] = jnp.sum(qm * kn_ref[0], axis=1, keepdims=True) * scale
        l_ref[...] = jnp.ones(l_ref.shape, jnp.float32)
        acc_ref[...] = jnp.broadcast_to(vn_ref[0], (N_HEADS, d_att))
        eye = (lax.broadcasted_iota(jnp.int32, (N_HEADS, N_HEADS), 0)
               == lax.broadcasted_iota(jnp.int32, (N_HEADS, N_HEADS), 1))
        d_ref[...] = jnp.sum(jnp.where(eye, jnp.broadcast_to(lfn_ref[0], (N_HEADS, N_HEADS)), 0.0),
                             axis=1, keepdims=True)

    lf = clf_ref[0, 0]
    suf = _lane_suffix_sum(lf)
    s = _bdot_nt(qm_ref[...], ck_ref[0, 0]) * scale
    s = s + d_ref[...] + (suf - lf)
    m_old = m_ref[...]
    m_new = jnp.maximum(m_old, jnp.max(s, axis=1, keepdims=True))
    p = jnp.exp(s - m_new)
    a = jnp.exp(m_old - m_new)
    l_ref[...] = a * l_ref[...] + jnp.sum(p, axis=1, keepdims=True)
    acc_ref[...] = a * acc_ref[...] + _bdot(p, cv_ref[0, 0])
    m_ref[...] = m_new
    d_ref[...] = d_ref[...] + suf[:, 0:1]

    @pl.when(j == pl.num_programs(1) - 1)
    def _():
        o = acc_ref[...] / l_ref[...]
        o_ref[0] = jnp.sum(jnp.where(own, o, 0.0), axis=0, keepdims=True)


def _fox_sample(q, k_new, v_new, lf_new, cache_k, cache_v, cache_lf_t, page_table, layer):
    n, d_att = q.shape
    n_pages = page_table.shape[1]
    per_seq = lambda shp: pl.BlockSpec((1,) + shp, lambda b, j, pt: (b, 0, 0))
    paged = lambda shp: pl.BlockSpec((1, 1) + shp, lambda b, j, pt: (pt[b, n_pages - 1 - j], layer, 0, 0))
    grid_spec = pltpu.PrefetchScalarGridSpec(
        num_scalar_prefetch=1,
        grid=(n, n_pages),
        in_specs=[per_seq((1, d_att)), per_seq((1, d_att)), per_seq((1, d_att)), per_seq((1, N_HEADS)),
                  paged((PAGE, d_att)), paged((PAGE, d_att)), paged((N_HEADS, PAGE))],
        out_specs=per_seq((1, d_att)),
        scratch_shapes=[pltpu.VMEM((N_HEADS, d_att), jnp.float32), pltpu.VMEM((N_HEADS, 1), jnp.float32),
                        pltpu.VMEM((N_HEADS, 1), jnp.float32), pltpu.VMEM((N_HEADS, d_att), jnp.float32),
                        pltpu.VMEM((N_HEADS, 1), jnp.float32)],
    )
    o = pl.pallas_call(
        functools.partial(_fox_sample_kernel, scale=HEAD_DIM ** -0.5),
        grid_spec=grid_spec,
        out_shape=jax.ShapeDtypeStruct((n, 1, d_att), jnp.float32),
        compiler_params=_params("arbitrary", "arbitrary"),
        name="fox_sample",
    )(page_table, q.reshape(n, 1, d_att), k_new.reshape(n, 1, d_att), v_new.reshape(n, 1, d_att),
      lf_new.reshape(n, 1, N_HEADS), cache_k, cache_v, cache_lf_t)
    return o.reshape(n, d_att)


def _sg_prompt_kernel(u_ref, v_ref, g_ref, b_ref, ws_ref, bs_ref, o_ref, *, tm):
    v = _layer_norm(_gelu(v_ref[...]), g_ref[...], b_ref[...])
    tri = (lax.broadcasted_iota(jnp.int32, (CHUNK, CHUNK), 1)
           <= lax.broadcasted_iota(jnp.int32, (CHUNK, CHUNK), 0))
    gd = v.shape[1] // SG_GROUPS
    for g in range(SG_GROUPS):
        w = jnp.where(tri, ws_ref[g], 0.0)
        bias = bs_ref[:, g:g + 1]
        cols = slice(g * gd, (g + 1) * gd)
        for c in range(tm // CHUNK):
            rows = slice(c * CHUNK, (c + 1) * CHUNK)
            y = _bdot(w, v[rows, cols]) + bias
            o_ref[rows, cols] = _gelu(u_ref[rows, cols]) * y


def _sg_prompt(hw, ln_g, ln_b, w_s, b_s_t):
    n, d2 = hw.shape
    d = d2 // 2
    tm = 512
    fixed = lambda shp: pl.BlockSpec(shp, lambda i: (0,) * len(shp))
    return pl.pallas_call(
        functools.partial(_sg_prompt_kernel, tm=tm),
        grid=(n // tm,),
        in_specs=[pl.BlockSpec((tm, d), lambda i: (i, 0)), pl.BlockSpec((tm, d), lambda i: (i, 1)),
                  fixed((1, d)), fixed((1, d)), fixed(w_s.shape), fixed(b_s_t.shape)],
        out_specs=pl.BlockSpec((tm, d), lambda i: (i, 0)),
        out_shape=jax.ShapeDtypeStruct((n, d), jnp.float32),
        compiler_params=_params("arbitrary"),
        name="sg_prompt",
    )(hw, hw, ln_g.reshape(1, d), ln_b.reshape(1, d), w_s, b_s_t)


def _sg_sample_kernel(u_ref, v_ref, g_ref, b_ref, wd_ref, b0_ref, o_ref, vo_ref):
    v = _layer_norm(_gelu(v_ref[...]), g_ref[...], b_ref[...])
    vo_ref[...] = v
    o_ref[...] = _gelu(u_ref[...]) * (wd_ref[...] * v + b0_ref[...])


def _sg_sample(hw, ln_g, ln_b, w_diag0, b0):
    n, d2 = hw.shape
    d = d2 // 2
    fixed = lambda shp: pl.BlockSpec(shp, lambda i: (0,) * len(shp))
    return pl.pallas_call(
        _sg_sample_kernel,
        grid=(1,),
        in_specs=[pl.BlockSpec((n, d), lambda i: (0, 0)), pl.BlockSpec((n, d), lambda i: (0, 1)),
                  fixed((1, d)), fixed((1, d)), fixed((1, d)), fixed((1, d))],
        out_specs=[fixed((n, d)), fixed((n, d))],
        out_shape=[jax.ShapeDtypeStruct((n, d), jnp.float32), jax.ShapeDtypeStruct((n, d), jnp.float32)],
        compiler_params=_params("arbitrary"),
        name="sg_sample",
    )(hw, hw, ln_g.reshape(1, d), ln_b.reshape(1, d), w_diag0.reshape(1, d), b0.reshape(1, d))


def _top16(s, ids):
    rows = s.shape[0]
    pos = lax.broadcasted_iota(jnp.int32, s.shape, 0)
    vals, outs = [], []
    for _ in range(PEER_TOPK):
        m = jnp.max(s, axis=0, keepdims=True)
        am = jnp.min(jnp.where(s == m, pos, rows), axis=0, keepdims=True)
        hit = pos == am
        vals.append(m)
        outs.append(am if ids is None else jnp.max(jnp.where(hit, ids, -1), axis=0, keepdims=True))
        s = jnp.where(hit, NEG_INF, s)
    return jnp.concatenate(vals, axis=0), jnp.concatenate(outs, axis=0)


def _peer_topk_kernel(q_ref, keys_ref, e_ref, g_ref):
    half = []
    for p in range(2):
        s = _bdot_nt(keys_ref[0, p], q_ref[:, p * PEER_NKEYS:(p + 1) * PEER_NKEYS])
        half.append(_top16(s, None))
    (s0, i0), (s1, i1) = half
    cand_s = jnp.concatenate([s0[a:a + 1, :] + s1 for a in range(PEER_TOPK)], axis=0)
    cand_i = jnp.concatenate([i0[a:a + 1, :] * PEER_NKEYS + i1 for a in range(PEER_TOPK)], axis=0)
    best_s, experts = _top16(cand_s, cand_i)
    e = jnp.exp(best_s - best_s[0:1, :])
    e_ref[0] = experts
    g_ref[0] = e / jnp.sum(e, axis=0, keepdims=True)


def _peer_topk(q, keys):
    n = q.shape[0]
    tb = _row_block(n, 256)
    dk = 2 * PEER_NKEYS
    out = pl.BlockSpec((1, PEER_TOPK, tb), lambda i, h: (h, 0, i))
    e, g = pl.pallas_call(
        _peer_topk_kernel,
        grid=(n // tb, PEER_HEADS),
        in_specs=[pl.BlockSpec((tb, dk), lambda i, h: (i, h)),
                  pl.BlockSpec((1, 2, PEER_NKEYS, dk // 2), lambda i, h: (h, 0, 0, 0))],
        out_specs=[out, out],
        out_shape=[jax.ShapeDtypeStruct((PEER_HEADS, PEER_TOPK, n), jnp.int32),
                   jax.ShapeDtypeStruct((PEER_HEADS, PEER_TOPK, n), jnp.float32)],
        compiler_params=_params("arbitrary", "arbitrary"),
        name="peer_topk",
    )(q, keys)
    return e.reshape(PEER_SLOTS, n), g.reshape(PEER_SLOTS, n)


def _tile_mean(y):
    return jnp.sum(jnp.sum(y, axis=1, keepdims=True), axis=0, keepdims=True) * (1.0 / y.size)


def _peer_gather_kernel(idx_ref, x_ref, g_ref, lg_ref, lb_ref, uv_ref, o_ref, buf_ref, sem_ref, *, tb, alpha):
    rows_per_expert = 2 * SUBLANES
    tok_rows = PEER_SLOTS * rows_per_expert
    n_groups = tb // GATHER_GROUP

    def issue(grp, slot):
        def body(j, c):
            for t in range(GATHER_GROUP):
                e = idx_ref[j, grp * GATHER_GROUP + t]
                dst = buf_ref.at[slot, pl.ds(t * tok_rows + j * rows_per_expert, rows_per_expert)]
                pltpu.make_async_copy(uv_ref.at[e], dst, sem_ref.at[slot]).start()
            return c
        lax.fori_loop(0, PEER_SLOTS, body, 0)

    def wait(slot):
        pltpu.make_async_copy(buf_ref.at[slot], buf_ref.at[slot], sem_ref.at[slot]).wait()

    def compute(grp, slot):
        for t in range(GATHER_GROUP):
            tok = grp * GATHER_GROUP + t
            x = x_ref[tok]
            acc = None
            for s in range(SUBLANES):
                u_s = buf_ref[slot, pl.ds(t * tok_rows + s, PEER_SLOTS, stride=rows_per_expert), :]
                term = u_s * x[s:s + 1, :]
                acc = term if acc is None else acc + term
            h = jnp.sum(acc, axis=1, keepdims=True)
            if tb > LANES:
                base = pl.multiple_of((tok // LANES) * LANES, LANES)
                gblk = g_ref[:, pl.ds(base, LANES)]
                tl = tok % LANES
            else:
                gblk = g_ref[...]
                tl = tok
            lane = lax.broadcasted_iota(jnp.int32, gblk.shape, 1)
            gate = jnp.sum(jnp.where(lane == tl, gblk, 0.0), axis=1, keepdims=True)
            w = gate * _gelu(h)
            rows = []
            for s in range(SUBLANES):
                v_s = buf_ref[slot, pl.ds(t * tok_rows + SUBLANES + s, PEER_SLOTS, stride=rows_per_expert), :]
                rows.append(jnp.sum(v_s * w, axis=0, keepdims=True))
            y = alpha * x + jnp.concatenate(rows, axis=0)
            mu = _tile_mean(y)
            yc = y - mu
            var = _tile_mean(yc * yc)
            o_ref[tok] = yc * lax.rsqrt(var + LN_EPS) * lg_ref[...] + lb_ref[...]

    issue(0, 0)

    def step(grp, c):
        slot = grp % 2

        @pl.when(grp + 1 < n_groups)
        def _():
            issue(grp + 1, 1 - slot)

        wait(slot)
        compute(grp, slot)
        return c

    lax.fori_loop(0, n_groups, step, 0)


def _peer_gather_ln(x, experts, gates, uv, ln_g, ln_b, alpha):
    n, d = x.shape
    tb = _row_block(n, 256)
    tile = (SUBLANES, d // SUBLANES)
    x3 = x.reshape((n,) + tile)
    out = pl.pallas_call(
        functools.partial(_peer_gather_kernel, tb=tb, alpha=alpha),
        grid=(n // tb,),
        in_specs=[pl.BlockSpec((PEER_SLOTS, tb), lambda i: (0, i), memory_space=pltpu.SMEM),
                  pl.BlockSpec((tb,) + tile, lambda i: (i, 0, 0)),
                  pl.BlockSpec((PEER_SLOTS, tb), lambda i: (0, i)),
                  pl.BlockSpec(tile, lambda i: (0, 0)), pl.BlockSpec(tile, lambda i: (0, 0)),
                  pl.BlockSpec(memory_space=pl.ANY)],
        out_specs=pl.BlockSpec((tb,) + tile, lambda i: (i, 0, 0)),
        out_shape=jax.ShapeDtypeStruct((n,) + tile, jnp.float32),
        scratch_shapes=[pltpu.VMEM((2, GATHER_GROUP * PEER_SLOTS * 2 * SUBLANES, d // SUBLANES), jnp.float32),
                        pltpu.SemaphoreType.DMA((2,))],
        compiler_params=_params("arbitrary", disable_bounds_checks=True),
        name="peer_gather",
    )(experts, x3, gates, ln_g.reshape(tile), ln_b.reshape(tile), uv)
    return out.reshape(n, d)


def kernel(x_prompt, x_sample, cache_k, cache_v, cache_logf, state_conv, page_table, w_in_ab, b_f, conv_w, w_o_ab,
           w_in_c, sg_ln_g, sg_ln_b, w_s, b_s, w_o_c, ln1_g, ln1_b, ln2_g, ln2_b, peer_wq, peer_keys, peer_u, peer_v):
    n_seq, t, d = x_prompt.shape
    n_dec = x_sample.shape[0]
    depth = ln1_g.shape[0]
    d_conv = conv_w.shape[2]
    d_att = N_HEADS * HEAD_DIM
    d_in = w_in_ab.shape[2]
    d_in_pad = -(-d_in // (5 * LANES)) * (5 * LANES)
    alpha = (2.0 * depth) ** 0.25
    n_exp = peer_u.shape[1]
    tile = (SUBLANES, d // SUBLANES)
    bf16 = jnp.bfloat16

    xp = x_prompt.reshape(n_seq * t, d)
    xs = x_sample.reshape(n_dec, d)
    n_phys = cache_k.shape[0]
    ck = cache_k.reshape(n_phys, -1, PAGE, d_att)
    cv = cache_v.reshape(n_phys, -1, PAGE, d_att)
    clf_t = jnp.swapaxes(cache_logf, 2, 3)

    kp_l, vp_l, fp_l, cp_l, ks_l, vs_l, fs_l, cs_l, sv_l = [], [], [], [], [], [], [], [], []
    for l in range(depth):
        i = l // 2
        if l % 2 == 0:
            w_in = jnp.pad(w_in_ab[i], ((0, 0), (0, d_in_pad - d_in))).astype(bf16)
            w_o = w_o_ab[i].astype(bf16)
            xw = _matmul(xp, w_in, d_in_pad // 5)
            a_out, lf, conv_new = _conv_prompt(xw, b_f[i], conv_w[i], n_seq, d_conv)
            cum_t = _cumsum_time(jnp.swapaxes(lf.reshape(n_seq, t, N_HEADS), 1, 2))
            cum = jnp.swapaxes(cum_t, 1, 2).reshape(n_seq * t, N_HEADS)
            o = _fox_prompt(xw, cum, cum_t, n_seq, d_conv)
            kp_l.append(xw[:, 3 * d_conv + d_att:3 * d_conv + 2 * d_att].reshape(n_seq, t, N_HEADS, HEAD_DIM))
            vp_l.append(xw[:, 3 * d_conv + 2 * d_att:3 * d_conv + 3 * d_att].reshape(n_seq, t, N_HEADS, HEAD_DIM))
            fp_l.append(lf.reshape(n_seq, t, N_HEADS))
            cp_l.append(conv_new)
            xp = _matmul_res_ln([a_out, o], [w_o[:d_conv], w_o[d_conv:]], xp, ln1_g[l], ln1_b[l], alpha)
            xw = _matmul(xs, w_in, d_in_pad // 5)
            a_out, lf, z = _conv_sample(xw, state_conv[:, i, 0], state_conv[:, i, 1], b_f[i], conv_w[i], d_conv)
            q = xw[:, 3 * d_conv:3 * d_conv + d_att]
            k = xw[:, 3 * d_conv + d_att:3 * d_conv + 2 * d_att]
            v = xw[:, 3 * d_conv + 2 * d_att:3 * d_conv + 3 * d_att]
            o = _fox_sample(q, k, v, lf, ck, cv, clf_t, page_table, i)
            ks_l.append(k.reshape(n_dec, 1, N_HEADS, HEAD_DIM))
            vs_l.append(v.reshape(n_dec, 1, N_HEADS, HEAD_DIM))
            fs_l.append(lf.reshape(n_dec, 1, N_HEADS))
            cs_l.append(jnp.stack([state_conv[:, i, 1], z], axis=1))
            xs = _matmul_res_ln([a_out, o], [w_o[:d_conv], w_o[d_conv:]], xs, ln1_g[l], ln1_b[l], alpha)
        else:
            w_in = w_in_c[i].astype(bf16)
            w_o = w_o_c[i].astype(bf16)
            d_sg = w_o.shape[0]
            hw = _matmul(xp, w_in, d_sg)
            sg = _sg_prompt(hw, sg_ln_g[i], sg_ln_b[i], w_s[i], b_s[i].T)
            xp = _matmul_res_ln([sg], [w_o], xp, ln1_g[l], ln1_b[l], alpha)
            hw = _matmul(xs, w_in, d_sg)
            gd = d_sg // SG_GROUPS
            sg, sv = _sg_sample(hw, sg_ln_g[i], sg_ln_b[i], jnp.repeat(w_s[i, :, 0, 0], gd), jnp.repeat(b_s[i, :, 0], gd))
            sv_l.append(sv.reshape(n_dec, 1, d_sg))
            xs = _matmul_res_ln([sg], [w_o], xs, ln1_g[l], ln1_b[l], alpha)
        wq = peer_wq[l].astype(bf16)
        uv = jnp.concatenate([peer_u[l].reshape((n_exp,) + tile), peer_v[l].reshape((n_exp,) + tile)], axis=1)
        for which in range(2):
            xx = xp if which == 0 else xs
            qq = _matmul(xx, wq, wq.shape[1] // 2)
            experts, gates = _peer_topk(qq, peer_keys[l])
            xx = _peer_gather_ln(xx, experts, gates, uv, ln2_g[l], ln2_b[l], alpha)
            if which == 0:
                xp = xx
            else:
                xs = xx

    return (xp.reshape(n_seq, t, d), xs.reshape(n_dec, 1, d),
            jnp.stack(kp_l, axis=1), jnp.stack(vp_l, axis=1), jnp.stack(fp_l, axis=1),
            jnp.stack(ks_l, axis=1), jnp.stack(vs_l, axis=1), jnp.stack(fs_l, axis=1),
            jnp.stack(cp_l, axis=1), jnp.stack(cs_l, axis=1), jnp.stack(sv_l, axis=1))
```

```python
import functools
import math

import jax
import jax.numpy as jnp
from jax import lax
from jax.experimental import pallas as pl
from jax.experimental.pallas import tpu as pltpu

LANES = 128
SUBLANES = 8
VMEM_LIMIT = 48 * 1024 * 1024

LN_EPS = 1e-5
HEAD_DIM = 64
N_HEADS = 8
PAGE = 128
CHUNK = 128
SG_GROUPS = 8
PEER_HEADS = 8
PEER_NKEYS = 128
PEER_TOPK = 16
PEER_SLOTS = PEER_HEADS * PEER_TOPK
GATHER_GROUP = 8
GATHER_BLOCK = 64
TOKENS_PER_ITER = 2
DECODE_PAGES_PER_STEP = 8
NEG_INF = float("-inf")


def _params(*sem, **kw):
    return pltpu.CompilerParams(dimension_semantics=sem, vmem_limit_bytes=VMEM_LIMIT, **kw)


def _gelu(x):
    return 0.5 * x * (1.0 + jnp.tanh(math.sqrt(2.0 / math.pi) * (x + 0.044715 * (x * x * x))))


def _layer_norm(x, g, b):
    mu = jnp.mean(x, axis=-1, keepdims=True)
    xc = x - mu
    var = jnp.mean(xc * xc, axis=-1, keepdims=True)
    return xc * lax.rsqrt(var + LN_EPS) * g + b


def _log_sigmoid(x):
    return -(jnp.maximum(-x, 0.0) + jnp.log1p(jnp.exp(-jnp.abs(x))))


def _bdot(a, b):
    return jnp.dot(a.astype(jnp.bfloat16), b.astype(jnp.bfloat16), preferred_element_type=jnp.float32)


def _bdot_nt(a, b):
    return lax.dot_general(a.astype(jnp.bfloat16), b.astype(jnp.bfloat16), (((1,), (1,)), ((), ())),
                           preferred_element_type=jnp.float32)


def _row_block(n, pref):
    return pref if n % pref == 0 else n


def _mm_kernel(x_ref, w_ref, o_ref):
    o_ref[...] = _bdot(x_ref[...], w_ref[...])


def _matmul(x, w, tn):
    n, k = x.shape
    m = w.shape[1]
    tm = _row_block(n, 512)
    return pl.pallas_call(
        _mm_kernel,
        grid=(m // tn, n // tm),
        in_specs=[pl.BlockSpec((tm, k), lambda j, i: (i, 0)),
                  pl.BlockSpec((k, tn), lambda j, i: (0, j))],
        out_specs=pl.BlockSpec((tm, tn), lambda j, i: (i, j)),
        out_shape=jax.ShapeDtypeStruct((n, m), jnp.float32),
        compiler_params=_params("arbitrary", "arbitrary"),
        name="matmul",
    )(x, w)


def _mm_res_ln_kernel(*refs, n_in, alpha):
    xs, ws = refs[:n_in], refs[n_in:2 * n_in]
    res_ref, g_ref, b_ref, o_ref = refs[2 * n_in:]
    y = alpha * res_ref[...]
    for x_ref, w_ref in zip(xs, ws):
        y = y + _bdot(x_ref[...], w_ref[...])
    o_ref[...] = _layer_norm(y, g_ref[...], b_ref[...])


def _matmul_res_ln(xs, ws, res, g, b, alpha):
    n, d = res.shape
    tm = _row_block(n, 512)
    row = lambda i: (i, 0)
    fixed = lambda i: (0, 0)
    in_specs = ([pl.BlockSpec((tm, x.shape[1]), row) for x in xs]
                + [pl.BlockSpec(w.shape, fixed) for w in ws]
                + [pl.BlockSpec((tm, d), row), pl.BlockSpec((1, d), fixed), pl.BlockSpec((1, d), fixed)])
    return pl.pallas_call(
        functools.partial(_mm_res_ln_kernel, n_in=len(xs), alpha=alpha),
        grid=(n // tm,),
        in_specs=in_specs,
        out_specs=pl.BlockSpec((tm, d), row),
        out_shape=jax.ShapeDtypeStruct((n, d), jnp.float32),
        compiler_params=_params("arbitrary"),
        name="matmul_res_ln",
    )(*xs, *ws, res, g.reshape(1, d), b.reshape(1, d))


def _conv_prompt_kernel(gb_ref, gc_ref, hc_ref, gcp_ref, hcp_ref, fl_ref, bf_ref, cw_ref,
                        a_ref, lf_ref, cn_ref, zs_ref, *, tm, blocks_per_seq):
    i = pl.program_id(0)
    z = gc_ref[...] * hc_ref[...]
    zprev = gcp_ref[...] * hcp_ref[...]
    zprev = jnp.where(i % blocks_per_seq == 0, jnp.zeros_like(zprev), zprev)
    zs_ref[0:SUBLANES, :] = zprev
    zs_ref[SUBLANES:, :] = z
    z1 = zs_ref[pl.ds(SUBLANES - 1, tm), :]
    z2 = zs_ref[pl.ds(SUBLANES - 2, tm), :]
    cw = cw_ref[...]
    y = cw[0:1, :] * z2 + cw[1:2, :] * z1 + cw[2:3, :] * z
    a_ref[...] = gb_ref[...] * y
    lf_ref[...] = _log_sigmoid(fl_ref[:, 0:N_HEADS] + bf_ref[...])
    cn_ref[0] = zs_ref[pl.ds(tm + SUBLANES - 2, 2), :]


def _conv_prompt(xw, b_f, conv_w, n_seq, d_conv):
    n = xw.shape[0]
    t = n // n_seq
    tm = 512
    bps = t // tm
    fl_blk = (6 * d_conv) // LANES
    col = lambda c: (lambda i: (i, c))
    prev = lambda c: (lambda i: (jnp.maximum(i * (tm // SUBLANES) - 1, 0), c))
    return pl.pallas_call(
        functools.partial(_conv_prompt_kernel, tm=tm, blocks_per_seq=bps),
        grid=(n // tm,),
        in_specs=[pl.BlockSpec((tm, d_conv), col(0)), pl.BlockSpec((tm, d_conv), col(1)),
                  pl.BlockSpec((tm, d_conv), col(2)),
                  pl.BlockSpec((SUBLANES, d_conv), prev(1)), pl.BlockSpec((SUBLANES, d_conv), prev(2)),
                  pl.BlockSpec((tm, LANES), col(fl_blk)),
                  pl.BlockSpec((1, N_HEADS), lambda i: (0, 0)),
                  pl.BlockSpec(conv_w.shape, lambda i: (0, 0))],
        out_specs=[pl.BlockSpec((tm, d_conv), lambda i: (i, 0)),
                   pl.BlockSpec((tm, N_HEADS), lambda i: (i, 0)),
                   pl.BlockSpec((1, 2, d_conv), lambda i: (i // bps, 0, 0))],
        out_shape=[jax.ShapeDtypeStruct((n, d_conv), jnp.float32),
                   jax.ShapeDtypeStruct((n, N_HEADS), jnp.float32),
                   jax.ShapeDtypeStruct((n_seq, 2, d_conv), jnp.float32)],
        scratch_shapes=[pltpu.VMEM((tm + SUBLANES, d_conv), jnp.float32)],
        compiler_params=_params("arbitrary"),
        name="conv_prompt",
    )(xw, xw, xw, xw, xw, xw, b_f.reshape(1, N_HEADS), conv_w)


def _conv_sample_kernel(gb_ref, gc_ref, hc_ref, fl_ref, p0_ref, p1_ref, bf_ref, cw_ref, a_ref, lf_ref, z_ref):
    z = gc_ref[...] * hc_ref[...]
    cw = cw_ref[...]
    y = cw[0:1, :] * p0_ref[...] + cw[1:2, :] * p1_ref[...] + cw[2:3, :] * z
    a_ref[...] = gb_ref[...] * y
    lf_ref[...] = _log_sigmoid(fl_ref[:, 0:N_HEADS] + bf_ref[...])
    z_ref[...] = z


def _conv_sample(xw, prev0, prev1, b_f, conv_w, d_conv):
    n = xw.shape[0]
    fl_blk = (6 * d_conv) // LANES
    full = lambda shp: pl.BlockSpec(shp, lambda i: (0,) * len(shp))
    return pl.pallas_call(
        _conv_sample_kernel,
        grid=(1,),
        in_specs=[pl.BlockSpec((n, d_conv), lambda i: (0, 0)), pl.BlockSpec((n, d_conv), lambda i: (0, 1)),
                  pl.BlockSpec((n, d_conv), lambda i: (0, 2)), pl.BlockSpec((n, LANES), lambda i: (0, fl_blk)),
                  full((n, d_conv)), full((n, d_conv)), full((1, N_HEADS)), full(conv_w.shape)],
        out_specs=[full((n, d_conv)), full((n, N_HEADS)), full((n, d_conv))],
        out_shape=[jax.ShapeDtypeStruct((n, d_conv), jnp.float32),
                   jax.ShapeDtypeStruct((n, N_HEADS), jnp.float32),
                   jax.ShapeDtypeStruct((n, d_conv), jnp.float32)],
        compiler_params=_params("arbitrary"),
        name="conv_sample",
    )(xw, xw, xw, xw, prev0, prev1, b_f.reshape(1, N_HEADS), conv_w)


def _lane_prefix_sum(x):
    lane = lax.broadcasted_iota(jnp.int32, x.shape, 1)
    d = 1
    while d < LANES:
        x = x + jnp.where(lane >= d, pltpu.roll(x, d, 1), 0.0)
        d *= 2
    return x


def _lane_suffix_sum(x):
    lane = lax.broadcasted_iota(jnp.int32, x.shape, 1)
    d = 1
    while d < LANES:
        x = x + jnp.where(lane < LANES - d, pltpu.roll(x, LANES - d, 1), 0.0)
        d *= 2
    return x


def _cumsum_kernel(x_ref, o_ref, *, t):
    carry = jnp.zeros((N_HEADS, 1), jnp.float32)
    for c in range(t // LANES):
        sl = slice(c * LANES, (c + 1) * LANES)
        s = _lane_prefix_sum(x_ref[0, :, sl]) + carry
        o_ref[0, :, sl] = s
        carry = s[:, LANES - 1:LANES]


def _cumsum_time(lf_t):
    n_seq, h, t = lf_t.shape
    return pl.pallas_call(
        functools.partial(_cumsum_kernel, t=t),
        grid=(n_seq,),
        in_specs=[pl.BlockSpec((1, h, t), lambda b: (b, 0, 0))],
        out_specs=pl.BlockSpec((1, h, t), lambda b: (b, 0, 0)),
        out_shape=jax.ShapeDtypeStruct(lf_t.shape, jnp.float32),
        compiler_params=_params("arbitrary"),
        name="cumsum_time",
    )(lf_t)


def _fox_prompt_kernel(q_ref, k_ref, v_ref, cq_ref, ck_ref, o_ref, m_ref, l_ref, acc_ref, *, tq, scale):
    qb, kb = pl.program_id(1), pl.program_id(2)

    @pl.when(kb == 0)
    def _():
        m_ref[...] = jnp.full(m_ref.shape, NEG_INF, jnp.float32)
        l_ref[...] = jnp.zeros(l_ref.shape, jnp.float32)
        acc_ref[...] = jnp.zeros(acc_ref.shape, jnp.float32)

    @pl.when(kb <= qb)
    def _():
        row = qb * tq + lax.broadcasted_iota(jnp.int32, (tq, tq), 0)
        colp = kb * tq + lax.broadcasted_iota(jnp.int32, (tq, tq), 1)
        causal = colp <= row
        lane = lax.broadcasted_iota(jnp.int32, (1, LANES), 1)
        for h in range(N_HEADS):
            pair = slice((h // 2) * LANES, (h // 2 + 1) * LANES)
            own = (lane // HEAD_DIM) == (h % 2)
            qh = jnp.where(own, q_ref[:, pair], 0.0)
            s = _bdot_nt(qh, k_ref[:, pair]) * scale
            s = s + cq_ref[:, h:h + 1] - ck_ref[0, h:h + 1, :]
            s = jnp.where(causal, s, NEG_INF)
            m_old = m_ref[h]
            m_new = jnp.maximum(m_old, jnp.max(s, axis=1, keepdims=True))
            p = jnp.exp(s - m_new)
            a = jnp.exp(m_old - m_new)
            l_ref[h] = a * l_ref[h] + jnp.sum(p, axis=1, keepdims=True)
            acc_ref[h] = a * acc_ref[h] + _bdot(p, v_ref[:, pair])
            m_ref[h] = m_new

    @pl.when(kb == qb)
    def _():
        lane = lax.broadcasted_iota(jnp.int32, (1, LANES), 1)
        for hp in range(N_HEADS // 2):
            o0 = acc_ref[2 * hp] / l_ref[2 * hp]
            o1 = acc_ref[2 * hp + 1] / l_ref[2 * hp + 1]
            o_ref[:, hp * LANES:(hp + 1) * LANES] = jnp.where(lane < HEAD_DIM, o0, o1)


def _fox_prompt(xw, cum, cum_t, n_seq, d_conv):
    n = xw.shape[0]
    t = n // n_seq
    d_att = N_HEADS * HEAD_DIM
    tq = 512
    nb = t // tq
    qc, kc, vc = (3 * d_conv) // d_att, (3 * d_conv) // d_att + 1, (3 * d_conv) // d_att + 2
    return pl.pallas_call(
        functools.partial(_fox_prompt_kernel, tq=tq, scale=HEAD_DIM ** -0.5),
        grid=(n_seq, nb, nb),
        in_specs=[pl.BlockSpec((tq, d_att), lambda b, i, j: (b * nb + i, qc)),
                  pl.BlockSpec((tq, d_att), lambda b, i, j: (b * nb + jnp.minimum(i, j), kc)),
                  pl.BlockSpec((tq, d_att), lambda b, i, j: (b * nb + jnp.minimum(i, j), vc)),
                  pl.BlockSpec((tq, N_HEADS), lambda b, i, j: (b * nb + i, 0)),
                  pl.BlockSpec((1, N_HEADS, tq), lambda b, i, j: (b, 0, jnp.minimum(i, j)))],
        out_specs=pl.BlockSpec((tq, d_att), lambda b, i, j: (b * nb + i, 0)),
        out_shape=jax.ShapeDtypeStruct((n, d_att), jnp.float32),
        scratch_shapes=[pltpu.VMEM((N_HEADS, tq, 1), jnp.float32), pltpu.VMEM((N_HEADS, tq, 1), jnp.float32),
                        pltpu.VMEM((N_HEADS, tq, LANES), jnp.float32)],
        compiler_params=_params("arbitrary", "arbitrary", "arbitrary"),
        name="fox_prompt",
    )(xw, xw, xw, cum, cum_t)


def _fox_sample_kernel(pt_ref, q_ref, kn_ref, vn_ref, lfn_ref, *refs, scale, pp):
    ck_refs, cv_refs, clf_refs = refs[:pp], refs[pp:2 * pp], refs[2 * pp:3 * pp]
    o_ref, qm_ref, m_ref, l_ref, acc_ref, d_ref = refs[3 * pp:]
    j = pl.program_id(1)
    d_att = N_HEADS * HEAD_DIM
    head_of_lane = lax.broadcasted_iota(jnp.int32, (N_HEADS, d_att), 1) // HEAD_DIM
    head_of_row = lax.broadcasted_iota(jnp.int32, (N_HEADS, d_att), 0)
    own = head_of_lane == head_of_row

    @pl.when(j == 0)
    def _():
        qm = jnp.where(own, jnp.broadcast_to(q_ref[0], (N_HEADS, d_att)), 0.0)
        qm_ref[...] = qm
        m_ref[...] = jnp.sum(qm * kn_ref[0], axis=1, keepdims=True) * scale
        l_ref[...] = jnp.ones(l_ref.shape, jnp.float32)
        acc_ref[...] = jnp.broadcast_to(vn_ref[0], (N_HEADS, d_att))
        eye = (lax.broadcasted_iota(jnp.int32, (N_HEADS, N_HEADS), 0)
               == lax.broadcasted_iota(jnp.int32, (N_HEADS, N_HEADS), 1))
        d_ref[...] = jnp.sum(jnp.where(eye, jnp.broadcast_to(lfn_ref[0], (N_HEADS, N_HEADS)), 0.0),
                             axis=1, keepdims=True)

    decay = d_ref[...]
    qm = qm_ref[...]
    ss = []
    for r in range(pp):
        lf = clf_refs[r][0, 0]
        suf = _lane_suffix_sum(lf)
        ss.append(_bdot_nt(qm, ck_refs[r][0, 0]) * scale + (decay + (suf - lf)))
        decay = decay + suf[:, 0:1]
    d_ref[...] = decay
    m_old = m_ref[...]
    m_new = m_old
    for s in ss:
        m_new = jnp.maximum(m_new, jnp.max(s, axis=1, keepdims=True))
    a = jnp.exp(m_old - m_new)
    l_new = a * l_ref[...]
    acc = a * acc_ref[...]
    for r in range(pp):
        p = jnp.exp(ss[r] - m_new)
        l_new = l_new + jnp.sum(p, axis=1, keepdims=True)
        acc = acc + _bdot(p, cv_refs[r][0, 0])
    l_ref[...] = l_new
    acc_ref[...] = acc
    m_ref[...] = m_new

    @pl.when(j == pl.num_programs(1) - 1)
    def _():
        o = acc_ref[...] / l_ref[...]
        o_ref[0] = jnp.sum(jnp.where(own, o, 0.0), axis=0, keepdims=True)


def _fox_sample(q, k_new, v_new, lf_new, cache_k, cache_v, cache_lf_t, page_table, layer):
    n, d_att = q.shape
    n_pages = page_table.shape[1]
    pp = math.gcd(n_pages, DECODE_PAGES_PER_STEP)
    per_seq = lambda shp: pl.BlockSpec((1,) + shp, lambda b, j, pt: (b, 0, 0))

    def paged(shp, r):
        return pl.BlockSpec((1, 1) + shp, lambda b, j, pt: (pt[b, n_pages - 1 - (j * pp + r)], layer, 0, 0))

    grid_spec = pltpu.PrefetchScalarGridSpec(
        num_scalar_prefetch=1,
        grid=(n, n_pages // pp),
        in_specs=([per_seq((1, d_att)), per_seq((1, d_att)), per_seq((1, d_att)), per_seq((1, N_HEADS))]
                  + [paged((PAGE, d_att), r) for r in range(pp)] + [paged((PAGE, d_att), r) for r in range(pp)]
                  + [paged((N_HEADS, PAGE), r) for r in range(pp)]),
        out_specs=per_seq((1, d_att)),
        scratch_shapes=[pltpu.VMEM((N_HEADS, d_att), jnp.float32), pltpu.VMEM((N_HEADS, 1), jnp.float32),
                        pltpu.VMEM((N_HEADS, 1), jnp.float32), pltpu.VMEM((N_HEADS, d_att), jnp.float32),
                        pltpu.VMEM((N_HEADS, 1), jnp.float32)],
    )
    o = pl.pallas_call(
        functools.partial(_fox_sample_kernel, scale=HEAD_DIM ** -0.5, pp=pp),
        grid_spec=grid_spec,
        out_shape=jax.ShapeDtypeStruct((n, 1, d_att), jnp.float32),
        compiler_params=_params("arbitrary", "arbitrary"),
        name="fox_sample",
    )(page_table, q.reshape(n, 1, d_att), k_new.reshape(n, 1, d_att), v_new.reshape(n, 1, d_att),
      lf_new.reshape(n, 1, N_HEADS), *([cache_k] * pp), *([cache_v] * pp), *([cache_lf_t] * pp))
    return o.reshape(n, d_att)


def _sg_prompt_kernel(u_ref, v_ref, g_ref, b_ref, ws_ref, bs_ref, o_ref, *, tm):
    v = _layer_norm(_gelu(v_ref[...]), g_ref[...], b_ref[...])
    tri = (lax.broadcasted_iota(jnp.int32, (CHUNK, CHUNK), 1)
           <= lax.broadcasted_iota(jnp.int32, (CHUNK, CHUNK), 0))
    gd = v.shape[1] // SG_GROUPS
    for g in range(SG_GROUPS):
        w = jnp.where(tri, ws_ref[g], 0.0)
        bias = bs_ref[:, g:g + 1]
        cols = slice(g * gd, (g + 1) * gd)
        for c in range(tm // CHUNK):
            rows = slice(c * CHUNK, (c + 1) * CHUNK)
            y = _bdot(w, v[rows, cols]) + bias
            o_ref[rows, cols] = _gelu(u_ref[rows, cols]) * y


def _sg_prompt(hw, ln_g, ln_b, w_s, b_s_t):
    n, d2 = hw.shape
    d = d2 // 2
    tm = 512
    fixed = lambda shp: pl.BlockSpec(shp, lambda i: (0,) * len(shp))
    return pl.pallas_call(
        functools.partial(_sg_prompt_kernel, tm=tm),
        grid=(n // tm,),
        in_specs=[pl.BlockSpec((tm, d), lambda i: (i, 0)), pl.BlockSpec((tm, d), lambda i: (i, 1)),
                  fixed((1, d)), fixed((1, d)), fixed(w_s.shape), fixed(b_s_t.shape)],
        out_specs=pl.BlockSpec((tm, d), lambda i: (i, 0)),
        out_shape=jax.ShapeDtypeStruct((n, d), jnp.float32),
        compiler_params=_params("arbitrary"),
        name="sg_prompt",
    )(hw, hw, ln_g.reshape(1, d), ln_b.reshape(1, d), w_s, b_s_t)


def _sg_sample_kernel(u_ref, v_ref, g_ref, b_ref, wd_ref, b0_ref, o_ref, vo_ref):
    v = _layer_norm(_gelu(v_ref[...]), g_ref[...], b_ref[...])
    vo_ref[...] = v
    o_ref[...] = _gelu(u_ref[...]) * (wd_ref[...] * v + b0_ref[...])


def _sg_sample(hw, ln_g, ln_b, w_diag0, b0):
    n, d2 = hw.shape
    d = d2 // 2
    fixed = lambda shp: pl.BlockSpec(shp, lambda i: (0,) * len(shp))
    return pl.pallas_call(
        _sg_sample_kernel,
        grid=(1,),
        in_specs=[pl.BlockSpec((n, d), lambda i: (0, 0)), pl.BlockSpec((n, d), lambda i: (0, 1)),
                  fixed((1, d)), fixed((1, d)), fixed((1, d)), fixed((1, d))],
        out_specs=[fixed((n, d)), fixed((n, d))],
        out_shape=[jax.ShapeDtypeStruct((n, d), jnp.float32), jax.ShapeDtypeStruct((n, d), jnp.float32)],
        compiler_params=_params("arbitrary"),
        name="sg_sample",
    )(hw, hw, ln_g.reshape(1, d), ln_b.reshape(1, d), w_diag0.reshape(1, d), b0.reshape(1, d))


def _top16(s, pos, ids):
    far = jnp.iinfo(jnp.int32).max
    vals, outs = [], []
    for _ in range(PEER_TOPK):
        m = jnp.max(s, axis=0, keepdims=True)
        am = jnp.min(jnp.where(s == m, pos, far), axis=0, keepdims=True)
        hit = pos == am
        vals.append(m)
        outs.append(am if ids is None else jnp.max(jnp.where(hit, ids, -1), axis=0, keepdims=True))
        s = jnp.where(hit, NEG_INF, s)
    return jnp.concatenate(vals, axis=0), jnp.concatenate(outs, axis=0)


def _peer_topk_kernel(q_ref, keys_ref, e_ref, g_ref):
    tb = q_ref.shape[0]
    key_pos = lax.broadcasted_iota(jnp.int32, (PEER_NKEYS, tb), 0)
    half = []
    for p in range(2):
        s = _bdot_nt(keys_ref[0, p], q_ref[:, p * PEER_NKEYS:(p + 1) * PEER_NKEYS])
        half.append(_top16(s, key_pos, None))
    (s0, i0), (s1, i1) = half
    row8 = lax.broadcasted_iota(jnp.int32, (SUBLANES, tb), 0)
    row16 = lax.broadcasted_iota(jnp.int32, (PEER_TOPK, tb), 0)
    cs, ci, cp = [s0[0:1, :] + s1], [i0[0:1, :] * PEER_NKEYS + i1], [row16]
    for a in range(1, SUBLANES):
        keep = row8 < PEER_TOPK // (a + 1)
        cs.append(jnp.where(keep, s0[a:a + 1, :] + s1[0:SUBLANES, :], NEG_INF))
        ci.append(i0[a:a + 1, :] * PEER_NKEYS + i1[0:SUBLANES, :])
        cp.append(jnp.where(keep, a * PEER_TOPK + row8, PEER_TOPK * PEER_TOPK + a * PEER_TOPK + row8))
    cs.append(s0[SUBLANES:, :] + s1[0:1, :])
    ci.append(i0[SUBLANES:, :] * PEER_NKEYS + i1[0:1, :])
    cp.append((row8 + SUBLANES) * PEER_TOPK)
    best_s, experts = _top16(jnp.concatenate(cs, axis=0), jnp.concatenate(cp, axis=0), jnp.concatenate(ci, axis=0))
    e = jnp.exp(best_s - best_s[0:1, :])
    e_ref[0] = experts
    g_ref[0] = e / jnp.sum(e, axis=0, keepdims=True)


def _peer_topk(q, keys):
    n = q.shape[0]
    tb = _row_block(n, 256)
    dk = 2 * PEER_NKEYS
    out = pl.BlockSpec((1, PEER_TOPK, tb), lambda i, h: (h, 0, i))
    e, g = pl.pallas_call(
        _peer_topk_kernel,
        grid=(n // tb, PEER_HEADS),
        in_specs=[pl.BlockSpec((tb, dk), lambda i, h: (i, h)),
                  pl.BlockSpec((1, 2, PEER_NKEYS, dk // 2), lambda i, h: (h, 0, 0, 0))],
        out_specs=[out, out],
        out_shape=[jax.ShapeDtypeStruct((PEER_HEADS, PEER_TOPK, n), jnp.int32),
                   jax.ShapeDtypeStruct((PEER_HEADS, PEER_TOPK, n), jnp.float32)],
        compiler_params=_params("arbitrary", "arbitrary"),
        name="peer_topk",
    )(q, keys)
    return e.reshape(PEER_SLOTS, n), g.reshape(PEER_SLOTS, n)


def _tree_sum(xs):
    while len(xs) > 1:
        xs = [xs[k] + xs[k + 1] for k in range(0, len(xs) - 1, 2)] + ([xs[-1]] if len(xs) % 2 else [])
    return xs[0]


def _tile_mean(y):
    s = jnp.sum(jnp.sum(y, axis=-1, keepdims=True), axis=-2, keepdims=True)
    return s * (1.0 / (y.shape[-1] * y.shape[-2]))


def _fold_sublanes(a, b, r):
    low = (lax.broadcasted_iota(jnp.int32, a.shape, 0) % (2 * r)) < r
    return jnp.where(low, a, pltpu.roll(b, r, 0)) + jnp.where(low, pltpu.roll(a, SUBLANES - r, 0), b)


_FOLD_ORDER = (0, 4, 2, 6, 1, 5, 3, 7)


def _sublane_sums(tiles):
    v = [tiles[_FOLD_ORDER.index(k)] for k in range(SUBLANES)]
    r = SUBLANES // 2
    while len(v) > 1:
        v = [_fold_sublanes(v[2 * k], v[2 * k + 1], r) for k in range(len(v) // 2)]
        r //= 2
    return v[0]


def _peer_gather_kernel(idx_ref, x_ref, g_ref, lg_ref, lb_ref, uv_ref, o_ref, buf0_ref, buf1_ref, sem_ref,
                        *, tb, d, alpha):
    i = pl.program_id(0)
    n_groups = tb // GATHER_GROUP
    group_picks = GATHER_GROUP * PEER_SLOTS
    n_lane_tiles = d // LANES
    bufs = (buf0_ref, buf1_ref)

    def start(k, slot, row):
        pltpu.make_async_copy(uv_ref.at[idx_ref[k]], bufs[slot].at[row], sem_ref.at[slot]).start()

    def wait(slot):
        pltpu.make_async_copy(bufs[slot], bufs[slot], sem_ref.at[slot]).wait()

    def mix_token(tok, t, slot, pick0, slot_next):
        buf = bufs[slot]
        x = x_ref[tok]
        lane = lax.broadcasted_iota(jnp.int32, (SUBLANES, tb), 1)
        ys = []
        for r in range(PEER_SLOTS // SUBLANES):
            row0 = t * PEER_SLOTS + r * SUBLANES
            for jj in range(SUBLANES):
                start(pick0 + (row0 + jj), slot_next, row0 + jj)
            h = jnp.sum(_sublane_sums([buf[row0 + jj, 0:SUBLANES, :] * x for jj in range(SUBLANES)]),
                        axis=1, keepdims=True)
            gate = jnp.sum(jnp.where(lane == tok, g_ref[0, r * SUBLANES:(r + 1) * SUBLANES, :], 0.0),
                           axis=1, keepdims=True)
            w = jnp.broadcast_to(gate * _gelu(h), (SUBLANES, LANES))
            ys.append(_tree_sum([jnp.broadcast_to(w[jj:jj + 1, :], (SUBLANES, LANES)) * buf[row0 + jj, SUBLANES:, :]
                                 for jj in range(SUBLANES)]))
        o_ref[tok] = alpha * x + _tree_sum(ys)

    def mix_group(grp, slot, slot_next):
        wait(slot)

        def body(tt, c):
            for dt in range(TOKENS_PER_ITER):
                t = tt * TOKENS_PER_ITER + dt
                mix_token(grp * GATHER_GROUP + t, t, slot, (grp + 1) * group_picks, slot_next)
            return c
        lax.fori_loop(0, GATHER_GROUP // TOKENS_PER_ITER, body, 0)

    @pl.when(i == 0)
    def _():
        def body(k, c):
            start(k, 0, k)
            return c
        lax.fori_loop(0, group_picks, body, 0, unroll=8)

    def pair(gg, c):
        mix_group(2 * gg, 0, 1)
        mix_group(2 * gg + 1, 1, 0)
        return c
    lax.fori_loop(0, n_groups // 2, pair, 0)

    @pl.when(i == pl.num_programs(0) - 1)
    def _():
        wait(0)

    y = o_ref[...]
    yc = y - _tile_mean(y)
    o_ref[...] = yc * lax.rsqrt(_tile_mean(yc * yc) + LN_EPS) * lg_ref[...] + lb_ref[...]


def _peer_gather_ln(x, experts, gates, uv, ln_g, ln_b, alpha):
    n, d = x.shape
    tb = _row_block(n, GATHER_BLOCK)
    nblk = n // tb
    assert (tb // GATHER_GROUP) % 2 == 0 and d == SUBLANES * LANES
    group_picks = GATHER_GROUP * PEER_SLOTS
    picks = experts.T.reshape(nblk, tb * PEER_SLOTS)
    picks = jnp.concatenate([picks, jnp.roll(picks[:, :group_picks], -1, axis=0)], axis=1).reshape(-1)
    buf = pltpu.VMEM((group_picks, 2 * SUBLANES, LANES), jnp.float32)
    tile = (SUBLANES, LANES)
    out = pl.pallas_call(
        functools.partial(_peer_gather_kernel, tb=tb, d=d, alpha=alpha),
        grid=(nblk,),
        in_specs=[pl.BlockSpec((tb * PEER_SLOTS + group_picks,), lambda i: (i,), memory_space=pltpu.SMEM),
                  pl.BlockSpec((tb,) + tile, lambda i: (i, 0, 0)),
                  pl.BlockSpec((1, PEER_SLOTS, tb), lambda i: (i, 0, 0)),
                  pl.BlockSpec(tile, lambda i: (0, 0)), pl.BlockSpec(tile, lambda i: (0, 0)),
                  pl.BlockSpec(memory_space=pl.ANY)],
        out_specs=pl.BlockSpec((tb,) + tile, lambda i: (i, 0, 0)),
        out_shape=jax.ShapeDtypeStruct((n,) + tile, jnp.float32),
        scratch_shapes=[buf, buf, pltpu.SemaphoreType.DMA((2,))],
        compiler_params=_params("arbitrary", disable_bounds_checks=True),
        name="peer_gather",
    )(picks, x.reshape((n,) + tile), jnp.swapaxes(gates.reshape(PEER_SLOTS, nblk, tb), 0, 1),
      ln_g.reshape(tile), ln_b.reshape(tile), uv)
    return out.reshape(n, d)


def kernel(x_prompt, x_sample, cache_k, cache_v, cache_logf, state_conv, page_table, w_in_ab, b_f, conv_w, w_o_ab,
           w_in_c, sg_ln_g, sg_ln_b, w_s, b_s, w_o_c, ln1_g, ln1_b, ln2_g, ln2_b, peer_wq, peer_keys, peer_u, peer_v):
    n_seq, t, d = x_prompt.shape
    n_dec = x_sample.shape[0]
    depth = ln1_g.shape[0]
    d_conv = conv_w.shape[2]
    d_att = N_HEADS * HEAD_DIM
    d_in = w_in_ab.shape[2]
    d_in_pad = -(-d_in // (5 * LANES)) * (5 * LANES)
    alpha = (2.0 * depth) ** 0.25
    bf16 = jnp.bfloat16

    xp = x_prompt.reshape(n_seq * t, d)
    xs = x_sample.reshape(n_dec, d)
    n_phys = cache_k.shape[0]
    ck = cache_k.reshape(n_phys, -1, PAGE, d_att)
    cv = cache_v.reshape(n_phys, -1, PAGE, d_att)
    clf_t = jnp.swapaxes(cache_logf, 2, 3)

    kp_l, vp_l, fp_l, cp_l, ks_l, vs_l, fs_l, cs_l, sv_l = [], [], [], [], [], [], [], [], []
    for l in range(depth):
        i = l // 2
        if l % 2 == 0:
            w_in = jnp.pad(w_in_ab[i], ((0, 0), (0, d_in_pad - d_in))).astype(bf16)
            w_o = w_o_ab[i].astype(bf16)
            xw = _matmul(xp, w_in, d_in_pad // 5)
            a_out, lf, conv_new = _conv_prompt(xw, b_f[i], conv_w[i], n_seq, d_conv)
            cum_t = _cumsum_time(jnp.swapaxes(lf.reshape(n_seq, t, N_HEADS), 1, 2))
            cum = jnp.swapaxes(cum_t, 1, 2).reshape(n_seq * t, N_HEADS)
            o = _fox_prompt(xw, cum, cum_t, n_seq, d_conv)
            kp_l.append(xw[:, 3 * d_conv + d_att:3 * d_conv + 2 * d_att].reshape(n_seq, t, N_HEADS, HEAD_DIM))
            vp_l.append(xw[:, 3 * d_conv + 2 * d_att:3 * d_conv + 3 * d_att].reshape(n_seq, t, N_HEADS, HEAD_DIM))
            fp_l.append(lf.reshape(n_seq, t, N_HEADS))
            cp_l.append(conv_new)
            xp = _matmul_res_ln([a_out, o], [w_o[:d_conv], w_o[d_conv:]], xp, ln1_g[l], ln1_b[l], alpha)
            xw = _matmul(xs, w_in, d_in_pad // 5)
            a_out, lf, z = _conv_sample(xw, state_conv[:, i, 0], state_conv[:, i, 1], b_f[i], conv_w[i], d_conv)
            q = xw[:, 3 * d_conv:3 * d_conv + d_att]
            k = xw[:, 3 * d_conv + d_att:3 * d_conv + 2 * d_att]
            v = xw[:, 3 * d_conv + 2 * d_att:3 * d_conv + 3 * d_att]
            o = _fox_sample(q, k, v, lf, ck, cv, clf_t, page_table, i)
            ks_l.append(k.reshape(n_dec, 1, N_HEADS, HEAD_DIM))
            vs_l.append(v.reshape(n_dec, 1, N_HEADS, HEAD_DIM))
            fs_l.append(lf.reshape(n_dec, 1, N_HEADS))
            cs_l.append(jnp.stack([state_conv[:, i, 1], z], axis=1))
            xs = _matmul_res_ln([a_out, o], [w_o[:d_conv], w_o[d_conv:]], xs, ln1_g[l], ln1_b[l], alpha)
        else:
            w_in = w_in_c[i].astype(bf16)
            w_o = w_o_c[i].astype(bf16)
            d_sg = w_o.shape[0]
            hw = _matmul(xp, w_in, d_sg)
            sg = _sg_prompt(hw, sg_ln_g[i], sg_ln_b[i], w_s[i], b_s[i].T)
            xp = _matmul_res_ln([sg], [w_o], xp, ln1_g[l], ln1_b[l], alpha)
            hw = _matmul(xs, w_in, d_sg)
            gd = d_sg // SG_GROUPS
            sg, sv = _sg_sample(hw, sg_ln_g[i], sg_ln_b[i], jnp.repeat(w_s[i, :, 0, 0], gd), jnp.repeat(b_s[i, :, 0], gd))
            sv_l.append(sv.reshape(n_dec, 1, d_sg))
            xs = _matmul_res_ln([sg], [w_o], xs, ln1_g[l], ln1_b[l], alpha)
        wq = peer_wq[l].astype(bf16)
        uv = jnp.concatenate([peer_u[l].reshape(-1, SUBLANES, LANES), peer_v[l].reshape(-1, SUBLANES, LANES)], axis=1)
        for which in range(2):
            xx = xp if which == 0 else xs
            qq = _matmul(xx, wq, wq.shape[1] // 2)
            experts, gates = _peer_topk(qq, peer_keys[l])
            xx = _peer_gather_ln(xx, experts, gates, uv, ln2_g[l], ln2_b[l], alpha)
            if which == 0:
                xp = xx
            else:
                xs = xx

    return (xp.reshape(n_seq, t, d), xs.reshape(n_dec, 1, d),
            jnp.stack(kp_l, axis=1), jnp.stack(vp_l, axis=1), jnp.stack(fp_l, axis=1),
            jnp.stack(ks_l, axis=1), jnp.stack(vs_l, axis=1), jnp.stack(fs_l, axis=1),
            jnp.stack(cp_l, axis=1), jnp.stack(cs_l, axis=1), jnp.stack(sv_l, axis=1))
```

```python
import functools
import math

import jax
import jax.numpy as jnp
from jax import lax
from jax.experimental import pallas as pl
from jax.experimental.pallas import tpu as pltpu

LANES = 128
SUBLANES = 8
VMEM_LIMIT = 48 * 1024 * 1024

LN_EPS = 1e-5
HEAD_DIM = 64
N_HEADS = 8
PAGE = 128
CHUNK = 128
SG_GROUPS = 8
PEER_HEADS = 8
PEER_NKEYS = 128
PEER_TOPK = 16
PEER_SLOTS = PEER_HEADS * PEER_TOPK
GATHER_GROUP = 8
DMA_THREADS = 2
GATHER_BLOCK = 64
TOKENS_PER_ITER = 2
DECODE_PAGES_PER_STEP = 8
NEG_INF = float("-inf")


def _params(*sem, **kw):
    return pltpu.CompilerParams(dimension_semantics=sem, vmem_limit_bytes=VMEM_LIMIT, **kw)


def _gelu(x):
    return 0.5 * x * (1.0 + jnp.tanh(math.sqrt(2.0 / math.pi) * (x + 0.044715 * (x * x * x))))


def _layer_norm(x, g, b):
    mu = jnp.mean(x, axis=-1, keepdims=True)
    xc = x - mu
    var = jnp.mean(xc * xc, axis=-1, keepdims=True)
    return xc * lax.rsqrt(var + LN_EPS) * g + b


def _log_sigmoid(x):
    return -(jnp.maximum(-x, 0.0) + jnp.log1p(jnp.exp(-jnp.abs(x))))


def _bdot(a, b):
    return jnp.dot(a.astype(jnp.bfloat16), b.astype(jnp.bfloat16), preferred_element_type=jnp.float32)


def _bdot_nt(a, b):
    return lax.dot_general(a.astype(jnp.bfloat16), b.astype(jnp.bfloat16), (((1,), (1,)), ((), ())),
                           preferred_element_type=jnp.float32)


def _row_block(n, pref):
    return pref if n % pref == 0 else n


def _mm_kernel(x_ref, w_ref, o_ref):
    o_ref[...] = _bdot(x_ref[...], w_ref[...])


def _matmul(x, w, tn):
    n, k = x.shape
    m = w.shape[1]
    tm = _row_block(n, 512)
    return pl.pallas_call(
        _mm_kernel,
        grid=(m // tn, n // tm),
        in_specs=[pl.BlockSpec((tm, k), lambda j, i: (i, 0)),
                  pl.BlockSpec((k, tn), lambda j, i: (0, j))],
        out_specs=pl.BlockSpec((tm, tn), lambda j, i: (i, j)),
        out_shape=jax.ShapeDtypeStruct((n, m), jnp.float32),
        compiler_params=_params("arbitrary", "arbitrary"),
        name="matmul",
    )(x, w)


def _mm_res_ln_kernel(*refs, n_in, alpha):
    xs, ws = refs[:n_in], refs[n_in:2 * n_in]
    res_ref, g_ref, b_ref, o_ref = refs[2 * n_in:]
    y = alpha * res_ref[...]
    for x_ref, w_ref in zip(xs, ws):
        y = y + _bdot(x_ref[...], w_ref[...])
    o_ref[...] = _layer_norm(y, g_ref[...], b_ref[...])


def _matmul_res_ln(xs, ws, res, g, b, alpha):
    n, d = res.shape
    tm = _row_block(n, 512)
    row = lambda i: (i, 0)
    fixed = lambda i: (0, 0)
    in_specs = ([pl.BlockSpec((tm, x.shape[1]), row) for x in xs]
                + [pl.BlockSpec(w.shape, fixed) for w in ws]
                + [pl.BlockSpec((tm, d), row), pl.BlockSpec((1, d), fixed), pl.BlockSpec((1, d), fixed)])
    return pl.pallas_call(
        functools.partial(_mm_res_ln_kernel, n_in=len(xs), alpha=alpha),
        grid=(n // tm,),
        in_specs=in_specs,
        out_specs=pl.BlockSpec((tm, d), row),
        out_shape=jax.ShapeDtypeStruct((n, d), jnp.float32),
        compiler_params=_params("arbitrary"),
        name="matmul_res_ln",
    )(*xs, *ws, res, g.reshape(1, d), b.reshape(1, d))


def _conv_prompt_kernel(gb_ref, gc_ref, hc_ref, gcp_ref, hcp_ref, fl_ref, bf_ref, cw_ref,
                        a_ref, lf_ref, cn_ref, zs_ref, *, tm, blocks_per_seq):
    i = pl.program_id(0)
    z = gc_ref[...] * hc_ref[...]
    zprev = gcp_ref[...] * hcp_ref[...]
    zprev = jnp.where(i % blocks_per_seq == 0, jnp.zeros_like(zprev), zprev)
    zs_ref[0:SUBLANES, :] = zprev
    zs_ref[SUBLANES:, :] = z
    z1 = zs_ref[pl.ds(SUBLANES - 1, tm), :]
    z2 = zs_ref[pl.ds(SUBLANES - 2, tm), :]
    cw = cw_ref[...]
    y = cw[0:1, :] * z2 + cw[1:2, :] * z1 + cw[2:3, :] * z
    a_ref[...] = gb_ref[...] * y
    lf_ref[...] = _log_sigmoid(fl_ref[:, 0:N_HEADS] + bf_ref[...])
    cn_ref[0] = zs_ref[pl.ds(tm + SUBLANES - 2, 2), :]


def _conv_prompt(xw, b_f, conv_w, n_seq, d_conv):
    n = xw.shape[0]
    t = n // n_seq
    tm = 512
    bps = t // tm
    fl_blk = (6 * d_conv) // LANES
    col = lambda c: (lambda i: (i, c))
    prev = lambda c: (lambda i: (jnp.maximum(i * (tm // SUBLANES) - 1, 0), c))
    return pl.pallas_call(
        functools.partial(_conv_prompt_kernel, tm=tm, blocks_per_seq=bps),
        grid=(n // tm,),
        in_specs=[pl.BlockSpec((tm, d_conv), col(0)), pl.BlockSpec((tm, d_conv), col(1)),
                  pl.BlockSpec((tm, d_conv), col(2)),
                  pl.BlockSpec((SUBLANES, d_conv), prev(1)), pl.BlockSpec((SUBLANES, d_conv), prev(2)),
                  pl.BlockSpec((tm, LANES), col(fl_blk)),
                  pl.BlockSpec((1, N_HEADS), lambda i: (0, 0)),
                  pl.BlockSpec(conv_w.shape, lambda i: (0, 0))],
        out_specs=[pl.BlockSpec((tm, d_conv), lambda i: (i, 0)),
                   pl.BlockSpec((tm, N_HEADS), lambda i: (i, 0)),
                   pl.BlockSpec((1, 2, d_conv), lambda i: (i // bps, 0, 0))],
        out_shape=[jax.ShapeDtypeStruct((n, d_conv), jnp.float32),
                   jax.ShapeDtypeStruct((n, N_HEADS), jnp.float32),
                   jax.ShapeDtypeStruct((n_seq, 2, d_conv), jnp.float32)],
        scratch_shapes=[pltpu.VMEM((tm + SUBLANES, d_conv), jnp.float32)],
        compiler_params=_params("arbitrary"),
        name="conv_prompt",
    )(xw, xw, xw, xw, xw, xw, b_f.reshape(1, N_HEADS), conv_w)


def _conv_sample_kernel(gb_ref, gc_ref, hc_ref, fl_ref, p0_ref, p1_ref, bf_ref, cw_ref, a_ref, lf_ref, z_ref):
    z = gc_ref[...] * hc_ref[...]
    cw = cw_ref[...]
    y = cw[0:1, :] * p0_ref[...] + cw[1:2, :] * p1_ref[...] + cw[2:3, :] * z
    a_ref[...] = gb_ref[...] * y
    lf_ref[...] = _log_sigmoid(fl_ref[:, 0:N_HEADS] + bf_ref[...])
    z_ref[...] = z


def _conv_sample(xw, prev0, prev1, b_f, conv_w, d_conv):
    n = xw.shape[0]
    fl_blk = (6 * d_conv) // LANES
    full = lambda shp: pl.BlockSpec(shp, lambda i: (0,) * len(shp))
    return pl.pallas_call(
        _conv_sample_kernel,
        grid=(1,),
        in_specs=[pl.BlockSpec((n, d_conv), lambda i: (0, 0)), pl.BlockSpec((n, d_conv), lambda i: (0, 1)),
                  pl.BlockSpec((n, d_conv), lambda i: (0, 2)), pl.BlockSpec((n, LANES), lambda i: (0, fl_blk)),
                  full((n, d_conv)), full((n, d_conv)), full((1, N_HEADS)), full(conv_w.shape)],
        out_specs=[full((n, d_conv)), full((n, N_HEADS)), full((n, d_conv))],
        out_shape=[jax.ShapeDtypeStruct((n, d_conv), jnp.float32),
                   jax.ShapeDtypeStruct((n, N_HEADS), jnp.float32),
                   jax.ShapeDtypeStruct((n, d_conv), jnp.float32)],
        compiler_params=_params("arbitrary"),
        name="conv_sample",
    )(xw, xw, xw, xw, prev0, prev1, b_f.reshape(1, N_HEADS), conv_w)


def _lane_prefix_sum(x):
    lane = lax.broadcasted_iota(jnp.int32, x.shape, 1)
    d = 1
    while d < LANES:
        x = x + jnp.where(lane >= d, pltpu.roll(x, d, 1), 0.0)
        d *= 2
    return x


def _lane_suffix_sum(x):
    lane = lax.broadcasted_iota(jnp.int32, x.shape, 1)
    d = 1
    while d < LANES:
        x = x + jnp.where(lane < LANES - d, pltpu.roll(x, LANES - d, 1), 0.0)
        d *= 2
    return x


def _cumsum_kernel(x_ref, o_ref, *, t):
    carry = jnp.zeros((N_HEADS, 1), jnp.float32)
    for c in range(t // LANES):
        sl = slice(c * LANES, (c + 1) * LANES)
        s = _lane_prefix_sum(x_ref[0, :, sl]) + carry
        o_ref[0, :, sl] = s
        carry = s[:, LANES - 1:LANES]


def _cumsum_time(lf_t):
    n_seq, h, t = lf_t.shape
    return pl.pallas_call(
        functools.partial(_cumsum_kernel, t=t),
        grid=(n_seq,),
        in_specs=[pl.BlockSpec((1, h, t), lambda b: (b, 0, 0))],
        out_specs=pl.BlockSpec((1, h, t), lambda b: (b, 0, 0)),
        out_shape=jax.ShapeDtypeStruct(lf_t.shape, jnp.float32),
        compiler_params=_params("arbitrary"),
        name="cumsum_time",
    )(lf_t)


def _fox_prompt_kernel(q_ref, k_ref, v_ref, cq_ref, ck_ref, o_ref, m_ref, l_ref, acc_ref, *, tq, scale):
    qb, kb = pl.program_id(1), pl.program_id(2)

    @pl.when(kb == 0)
    def _():
        m_ref[...] = jnp.full(m_ref.shape, NEG_INF, jnp.float32)
        l_ref[...] = jnp.zeros(l_ref.shape, jnp.float32)
        acc_ref[...] = jnp.zeros(acc_ref.shape, jnp.float32)

    @pl.when(kb <= qb)
    def _():
        row = qb * tq + lax.broadcasted_iota(jnp.int32, (tq, tq), 0)
        colp = kb * tq + lax.broadcasted_iota(jnp.int32, (tq, tq), 1)
        causal = colp <= row
        lane = lax.broadcasted_iota(jnp.int32, (1, LANES), 1)
        for h in range(N_HEADS):
            pair = slice((h // 2) * LANES, (h // 2 + 1) * LANES)
            own = (lane // HEAD_DIM) == (h % 2)
            qh = jnp.where(own, q_ref[:, pair], 0.0)
            s = _bdot_nt(qh, k_ref[:, pair]) * scale
            s = s + cq_ref[:, h:h + 1] - ck_ref[0, h:h + 1, :]
            s = jnp.where(causal, s, NEG_INF)
            m_old = m_ref[h]
            m_new = jnp.maximum(m_old, jnp.max(s, axis=1, keepdims=True))
            p = jnp.exp(s - m_new)
            a = jnp.exp(m_old - m_new)
            l_ref[h] = a * l_ref[h] + jnp.sum(p, axis=1, keepdims=True)
            acc_ref[h] = a * acc_ref[h] + _bdot(p, v_ref[:, pair])
            m_ref[h] = m_new

    @pl.when(kb == qb)
    def _():
        lane = lax.broadcasted_iota(jnp.int32, (1, LANES), 1)
        for hp in range(N_HEADS // 2):
            o0 = acc_ref[2 * hp] / l_ref[2 * hp]
            o1 = acc_ref[2 * hp + 1] / l_ref[2 * hp + 1]
            o_ref[:, hp * LANES:(hp + 1) * LANES] = jnp.where(lane < HEAD_DIM, o0, o1)


def _fox_prompt(xw, cum, cum_t, n_seq, d_conv):
    n = xw.shape[0]
    t = n // n_seq
    d_att = N_HEADS * HEAD_DIM
    tq = 512
    nb = t // tq
    qc, kc, vc = (3 * d_conv) // d_att, (3 * d_conv) // d_att + 1, (3 * d_conv) // d_att + 2
    return pl.pallas_call(
        functools.partial(_fox_prompt_kernel, tq=tq, scale=HEAD_DIM ** -0.5),
        grid=(n_seq, nb, nb),
        in_specs=[pl.BlockSpec((tq, d_att), lambda b, i, j: (b * nb + i, qc)),
                  pl.BlockSpec((tq, d_att), lambda b, i, j: (b * nb + jnp.minimum(i, j), kc)),
                  pl.BlockSpec((tq, d_att), lambda b, i, j: (b * nb + jnp.minimum(i, j), vc)),
                  pl.BlockSpec((tq, N_HEADS), lambda b, i, j: (b * nb + i, 0)),
                  pl.BlockSpec((1, N_HEADS, tq), lambda b, i, j: (b, 0, jnp.minimum(i, j)))],
        out_specs=pl.BlockSpec((tq, d_att), lambda b, i, j: (b * nb + i, 0)),
        out_shape=jax.ShapeDtypeStruct((n, d_att), jnp.float32),
        scratch_shapes=[pltpu.VMEM((N_HEADS, tq, 1), jnp.float32), pltpu.VMEM((N_HEADS, tq, 1), jnp.float32),
                        pltpu.VMEM((N_HEADS, tq, LANES), jnp.float32)],
        compiler_params=_params("arbitrary", "arbitrary", "arbitrary"),
        name="fox_prompt",
    )(xw, xw, xw, cum, cum_t)


def _fox_sample_kernel(pt_ref, q_ref, kn_ref, vn_ref, lfn_ref, *refs, scale, pp):
    ck_refs, cv_refs, clf_refs = refs[:pp], refs[pp:2 * pp], refs[2 * pp:3 * pp]
    o_ref, qm_ref, m_ref, l_ref, acc_ref, d_ref = refs[3 * pp:]
    j = pl.program_id(1)
    d_att = N_HEADS * HEAD_DIM
    head_of_lane = lax.broadcasted_iota(jnp.int32, (N_HEADS, d_att), 1) // HEAD_DIM
    head_of_row = lax.broadcasted_iota(jnp.int32, (N_HEADS, d_att), 0)
    own = head_of_lane == head_of_row

    @pl.when(j == 0)
    def _():
        qm = jnp.where(own, jnp.broadcast_to(q_ref[0], (N_HEADS, d_att)), 0.0)
        qm_ref[...] = qm
        m_ref[...] = jnp.sum(qm * kn_ref[0], axis=1, keepdims=True) * scale
        l_ref[...] = jnp.ones(l_ref.shape, jnp.float32)
        acc_ref[...] = jnp.broadcast_to(vn_ref[0], (N_HEADS, d_att))
        eye = (lax.broadcasted_iota(jnp.int32, (N_HEADS, N_HEADS), 0)
               == lax.broadcasted_iota(jnp.int32, (N_HEADS, N_HEADS), 1))
        d_ref[...] = jnp.sum(jnp.where(eye, jnp.broadcast_to(lfn_ref[0], (N_HEADS, N_HEADS)), 0.0),
                             axis=1, keepdims=True)

    decay = d_ref[...]
    qm = qm_ref[...]
    ss = []
    for r in range(pp):
        lf = clf_refs[r][0, 0]
        suf = _lane_suffix_sum(lf)
        ss.append(_bdot_nt(qm, ck_refs[r][0, 0]) * scale + (decay + (suf - lf)))
        decay = decay + suf[:, 0:1]
    d_ref[...] = decay
    m_old = m_ref[...]
    m_new = m_old
    for s in ss:
        m_new = jnp.maximum(m_new, jnp.max(s, axis=1, keepdims=True))
    a = jnp.exp(m_old - m_new)
    l_new = a * l_ref[...]
    acc = a * acc_ref[...]
    for r in range(pp):
        p = jnp.exp(ss[r] - m_new)
        l_new = l_new + jnp.sum(p, axis=1, keepdims=True)
        acc = acc + _bdot(p, cv_refs[r][0, 0])
    l_ref[...] = l_new
    acc_ref[...] = acc
    m_ref[...] = m_new

    @pl.when(j == pl.num_programs(1) - 1)
    def _():
        o = acc_ref[...] / l_ref[...]
        o_ref[0] = jnp.sum(jnp.where(own, o, 0.0), axis=0, keepdims=True)


def _fox_sample(q, k_new, v_new, lf_new, cache_k, cache_v, cache_lf_t, page_table, layer):
    n, d_att = q.shape
    n_pages = page_table.shape[1]
    pp = math.gcd(n_pages, DECODE_PAGES_PER_STEP)
    per_seq = lambda shp: pl.BlockSpec((1,) + shp, lambda b, j, pt: (b, 0, 0))

    def paged(shp, r):
        return pl.BlockSpec((1, 1) + shp, lambda b, j, pt: (pt[b, n_pages - 1 - (j * pp + r)], layer, 0, 0))

    grid_spec = pltpu.PrefetchScalarGridSpec(
        num_scalar_prefetch=1,
        grid=(n, n_pages // pp),
        in_specs=([per_seq((1, d_att)), per_seq((1, d_att)), per_seq((1, d_att)), per_seq((1, N_HEADS))]
                  + [paged((PAGE, d_att), r) for r in range(pp)] + [paged((PAGE, d_att), r) for r in range(pp)]
                  + [paged((N_HEADS, PAGE), r) for r in range(pp)]),
        out_specs=per_seq((1, d_att)),
        scratch_shapes=[pltpu.VMEM((N_HEADS, d_att), jnp.float32), pltpu.VMEM((N_HEADS, 1), jnp.float32),
                        pltpu.VMEM((N_HEADS, 1), jnp.float32), pltpu.VMEM((N_HEADS, d_att), jnp.float32),
                        pltpu.VMEM((N_HEADS, 1), jnp.float32)],
    )
    o = pl.pallas_call(
        functools.partial(_fox_sample_kernel, scale=HEAD_DIM ** -0.5, pp=pp),
        grid_spec=grid_spec,
        out_shape=jax.ShapeDtypeStruct((n, 1, d_att), jnp.float32),
        compiler_params=_params("arbitrary", "arbitrary"),
        name="fox_sample",
    )(page_table, q.reshape(n, 1, d_att), k_new.reshape(n, 1, d_att), v_new.reshape(n, 1, d_att),
      lf_new.reshape(n, 1, N_HEADS), *([cache_k] * pp), *([cache_v] * pp), *([cache_lf_t] * pp))
    return o.reshape(n, d_att)


def _sg_prompt_kernel(u_ref, v_ref, g_ref, b_ref, ws_ref, bs_ref, o_ref, *, tm):
    v = _layer_norm(_gelu(v_ref[...]), g_ref[...], b_ref[...])
    tri = (lax.broadcasted_iota(jnp.int32, (CHUNK, CHUNK), 1)
           <= lax.broadcasted_iota(jnp.int32, (CHUNK, CHUNK), 0))
    gd = v.shape[1] // SG_GROUPS
    for g in range(SG_GROUPS):
        w = jnp.where(tri, ws_ref[g], 0.0)
        bias = bs_ref[:, g:g + 1]
        cols = slice(g * gd, (g + 1) * gd)
        for c in range(tm // CHUNK):
            rows = slice(c * CHUNK, (c + 1) * CHUNK)
            y = _bdot(w, v[rows, cols]) + bias
            o_ref[rows, cols] = _gelu(u_ref[rows, cols]) * y


def _sg_prompt(hw, ln_g, ln_b, w_s, b_s_t):
    n, d2 = hw.shape
    d = d2 // 2
    tm = 512
    fixed = lambda shp: pl.BlockSpec(shp, lambda i: (0,) * len(shp))
    return pl.pallas_call(
        functools.partial(_sg_prompt_kernel, tm=tm),
        grid=(n // tm,),
        in_specs=[pl.BlockSpec((tm, d), lambda i: (i, 0)), pl.BlockSpec((tm, d), lambda i: (i, 1)),
                  fixed((1, d)), fixed((1, d)), fixed(w_s.shape), fixed(b_s_t.shape)],
        out_specs=pl.BlockSpec((tm, d), lambda i: (i, 0)),
        out_shape=jax.ShapeDtypeStruct((n, d), jnp.float32),
        compiler_params=_params("arbitrary"),
        name="sg_prompt",
    )(hw, hw, ln_g.reshape(1, d), ln_b.reshape(1, d), w_s, b_s_t)


def _sg_sample_kernel(u_ref, v_ref, g_ref, b_ref, wd_ref, b0_ref, o_ref, vo_ref):
    v = _layer_norm(_gelu(v_ref[...]), g_ref[...], b_ref[...])
    vo_ref[...] = v
    o_ref[...] = _gelu(u_ref[...]) * (wd_ref[...] * v + b0_ref[...])


def _sg_sample(hw, ln_g, ln_b, w_diag0, b0):
    n, d2 = hw.shape
    d = d2 // 2
    fixed = lambda shp: pl.BlockSpec(shp, lambda i: (0,) * len(shp))
    return pl.pallas_call(
        _sg_sample_kernel,
        grid=(1,),
        in_specs=[pl.BlockSpec((n, d), lambda i: (0, 0)), pl.BlockSpec((n, d), lambda i: (0, 1)),
                  fixed((1, d)), fixed((1, d)), fixed((1, d)), fixed((1, d))],
        out_specs=[fixed((n, d)), fixed((n, d))],
        out_shape=[jax.ShapeDtypeStruct((n, d), jnp.float32), jax.ShapeDtypeStruct((n, d), jnp.float32)],
        compiler_params=_params("arbitrary"),
        name="sg_sample",
    )(hw, hw, ln_g.reshape(1, d), ln_b.reshape(1, d), w_diag0.reshape(1, d), b0.reshape(1, d))


def _top16(s, pos, ids):
    far = jnp.iinfo(jnp.int32).max
    vals, outs = [], []
    for _ in range(PEER_TOPK):
        m = jnp.max(s, axis=0, keepdims=True)
        am = jnp.min(jnp.where(s == m, pos, far), axis=0, keepdims=True)
        hit = pos == am
        vals.append(m)
        outs.append(am if ids is None else jnp.max(jnp.where(hit, ids, -1), axis=0, keepdims=True))
        s = jnp.where(hit, NEG_INF, s)
    return jnp.concatenate(vals, axis=0), jnp.concatenate(outs, axis=0)


def _peer_topk_kernel(q_ref, keys_ref, e_ref, g_ref):
    tb = q_ref.shape[0]
    key_pos = lax.broadcasted_iota(jnp.int32, (PEER_NKEYS, tb), 0)
    half = []
    for p in range(2):
        s = _bdot_nt(keys_ref[0, p], q_ref[:, p * PEER_NKEYS:(p + 1) * PEER_NKEYS])
        half.append(_top16(s, key_pos, None))
    (s0, i0), (s1, i1) = half
    row8 = lax.broadcasted_iota(jnp.int32, (SUBLANES, tb), 0)
    row16 = lax.broadcasted_iota(jnp.int32, (PEER_TOPK, tb), 0)
    cs, ci, cp = [s0[0:1, :] + s1], [i0[0:1, :] * PEER_NKEYS + i1], [row16]
    for a in range(1, SUBLANES):
        keep = row8 < PEER_TOPK // (a + 1)
        cs.append(jnp.where(keep, s0[a:a + 1, :] + s1[0:SUBLANES, :], NEG_INF))
        ci.append(i0[a:a + 1, :] * PEER_NKEYS + i1[0:SUBLANES, :])
        cp.append(jnp.where(keep, a * PEER_TOPK + row8, PEER_TOPK * PEER_TOPK + a * PEER_TOPK + row8))
    cs.append(s0[SUBLANES:, :] + s1[0:1, :])
    ci.append(i0[SUBLANES:, :] * PEER_NKEYS + i1[0:1, :])
    cp.append((row8 + SUBLANES) * PEER_TOPK)
    best_s, experts = _top16(jnp.concatenate(cs, axis=0), jnp.concatenate(cp, axis=0), jnp.concatenate(ci, axis=0))
    e = jnp.exp(best_s - best_s[0:1, :])
    e_ref[0] = experts
    g_ref[0] = e / jnp.sum(e, axis=0, keepdims=True)


def _peer_topk(q, keys):
    n = q.shape[0]
    tb = _row_block(n, 256)
    dk = 2 * PEER_NKEYS
    out = pl.BlockSpec((1, PEER_TOPK, tb), lambda i, h: (h, 0, i))
    e, g = pl.pallas_call(
        _peer_topk_kernel,
        grid=(n // tb, PEER_HEADS),
        in_specs=[pl.BlockSpec((tb, dk), lambda i, h: (i, h)),
                  pl.BlockSpec((1, 2, PEER_NKEYS, dk // 2), lambda i, h: (h, 0, 0, 0))],
        out_specs=[out, out],
        out_shape=[jax.ShapeDtypeStruct((PEER_HEADS, PEER_TOPK, n), jnp.int32),
                   jax.ShapeDtypeStruct((PEER_HEADS, PEER_TOPK, n), jnp.float32)],
        compiler_params=_params("arbitrary", "arbitrary"),
        name="peer_topk",
    )(q, keys)
    return e.reshape(PEER_SLOTS, n), g.reshape(PEER_SLOTS, n)


def _tree_sum(xs):
    while len(xs) > 1:
        xs = [xs[k] + xs[k + 1] for k in range(0, len(xs) - 1, 2)] + ([xs[-1]] if len(xs) % 2 else [])
    return xs[0]


def _tile_mean(y):
    s = jnp.sum(jnp.sum(y, axis=-1, keepdims=True), axis=-2, keepdims=True)
    return s * (1.0 / (y.shape[-1] * y.shape[-2]))


def _fold_sublanes(a, b, r):
    low = (lax.broadcasted_iota(jnp.int32, a.shape, 0) % (2 * r)) < r
    return jnp.where(low, a, pltpu.roll(b, r, 0)) + jnp.where(low, pltpu.roll(a, SUBLANES - r, 0), b)


_FOLD_ORDER = (0, 4, 2, 6, 1, 5, 3, 7)


def _sublane_sums(tiles):
    v = [tiles[_FOLD_ORDER.index(k)] for k in range(SUBLANES)]
    r = SUBLANES // 2
    while len(v) > 1:
        v = [_fold_sublanes(v[2 * k], v[2 * k + 1], r) for k in range(len(v) // 2)]
        r //= 2
    return v[0]


def _peer_gather_kernel(idx_ref, x_ref, g_ref, lg_ref, lb_ref, uv_ref, o_ref, buf0_ref, buf1_ref, sem_ref,
                        *, tb, d, alpha):
    i = pl.program_id(0)
    n_groups = tb // GATHER_GROUP
    group_picks = GATHER_GROUP * PEER_SLOTS
    n_lane_tiles = d // LANES
    bufs = (buf0_ref, buf1_ref)

    def start(k, slot, row, thread=0):
        pltpu.make_async_copy(uv_ref.at[idx_ref[k]], bufs[slot].at[row], sem_ref.at[slot]).start(priority=thread)

    def wait(slot):
        pltpu.make_async_copy(bufs[slot], bufs[slot], sem_ref.at[slot]).wait()

    def mix_token(tok, t, slot, pick0, slot_next):
        buf = bufs[slot]
        x = x_ref[tok]
        lane = lax.broadcasted_iota(jnp.int32, (SUBLANES, tb), 1)
        ys = []
        for r in range(PEER_SLOTS // SUBLANES):
            row0 = t * PEER_SLOTS + r * SUBLANES
            for jj in range(SUBLANES):
                start(pick0 + (row0 + jj), slot_next, row0 + jj, thread=jj % DMA_THREADS)
            words = [buf[row0 + jj] for jj in range(SUBLANES)]
            us = [lax.bitcast_convert_type(wd << 16, jnp.float32) for wd in words]
            vs = [lax.bitcast_convert_type(wd & jnp.uint32(0xFFFF0000), jnp.float32) for wd in words]
            h = jnp.sum(_sublane_sums([u * x for u in us]), axis=1, keepdims=True)
            gate = jnp.sum(jnp.where(lane == tok, g_ref[0, r * SUBLANES:(r + 1) * SUBLANES, :], 0.0),
                           axis=1, keepdims=True)
            w = jnp.broadcast_to(gate * _gelu(h), (SUBLANES, LANES))
            ys.append(_tree_sum([jnp.broadcast_to(w[jj:jj + 1, :], (SUBLANES, LANES)) * vs[jj]
                                 for jj in range(SUBLANES)]))
        o_ref[tok] = alpha * x + _tree_sum(ys)

    def mix_group(grp, slot, slot_next):
        wait(slot)

        def body(tt, c):
            for dt in range(TOKENS_PER_ITER):
                t = tt * TOKENS_PER_ITER + dt
                mix_token(grp * GATHER_GROUP + t, t, slot, (grp + 1) * group_picks, slot_next)
            return c
        lax.fori_loop(0, GATHER_GROUP // TOKENS_PER_ITER, body, 0)

    @pl.when(i == 0)
    def _():
        def body(k, c):
            start(k, 0, k)
            return c
        lax.fori_loop(0, group_picks, body, 0, unroll=8)

    def pair(gg, c):
        mix_group(2 * gg, 0, 1)
        mix_group(2 * gg + 1, 1, 0)
        return c
    lax.fori_loop(0, n_groups // 2, pair, 0)

    @pl.when(i == pl.num_programs(0) - 1)
    def _():
        wait(0)

    y = o_ref[...]
    yc = y - _tile_mean(y)
    o_ref[...] = yc * lax.rsqrt(_tile_mean(yc * yc) + LN_EPS) * lg_ref[...] + lb_ref[...]


def _pack_bf16_pair(lo, hi):
    bits = lambda a: lax.bitcast_convert_type(a.astype(jnp.bfloat16), jnp.uint16).astype(jnp.uint32)
    return bits(lo) | (bits(hi) << 16)


def _peer_gather_ln(x, experts, gates, uv, ln_g, ln_b, alpha):
    n, d = x.shape
    tb = _row_block(n, GATHER_BLOCK)
    nblk = n // tb
    assert (tb // GATHER_GROUP) % 2 == 0 and d == SUBLANES * LANES
    group_picks = GATHER_GROUP * PEER_SLOTS
    picks = experts.T.reshape(nblk, tb * PEER_SLOTS)
    picks = jnp.concatenate([picks, jnp.roll(picks[:, :group_picks], -1, axis=0)], axis=1).reshape(-1)
    buf = pltpu.VMEM((group_picks, SUBLANES, LANES), jnp.uint32)
    tile = (SUBLANES, LANES)
    out = pl.pallas_call(
        functools.partial(_peer_gather_kernel, tb=tb, d=d, alpha=alpha),
        grid=(nblk,),
        in_specs=[pl.BlockSpec((tb * PEER_SLOTS + group_picks,), lambda i: (i,), memory_space=pltpu.SMEM),
                  pl.BlockSpec((tb,) + tile, lambda i: (i, 0, 0)),
                  pl.BlockSpec((1, PEER_SLOTS, tb), lambda i: (i, 0, 0)),
                  pl.BlockSpec(tile, lambda i: (0, 0)), pl.BlockSpec(tile, lambda i: (0, 0)),
                  pl.BlockSpec(memory_space=pl.ANY)],
        out_specs=pl.BlockSpec((tb,) + tile, lambda i: (i, 0, 0)),
        out_shape=jax.ShapeDtypeStruct((n,) + tile, jnp.float32),
        scratch_shapes=[buf, buf, pltpu.SemaphoreType.DMA((2,))],
        compiler_params=_params("arbitrary", disable_bounds_checks=True),
        name="peer_gather",
    )(picks, x.reshape((n,) + tile), jnp.swapaxes(gates.reshape(PEER_SLOTS, nblk, tb), 0, 1),
      ln_g.reshape(tile), ln_b.reshape(tile), uv)
    return out.reshape(n, d)


def kernel(x_prompt, x_sample, cache_k, cache_v, cache_logf, state_conv, page_table, w_in_ab, b_f, conv_w, w_o_ab,
           w_in_c, sg_ln_g, sg_ln_b, w_s, b_s, w_o_c, ln1_g, ln1_b, ln2_g, ln2_b, peer_wq, peer_keys, peer_u, peer_v):
    n_seq, t, d = x_prompt.shape
    n_dec = x_sample.shape[0]
    depth = ln1_g.shape[0]
    d_conv = conv_w.shape[2]
    d_att = N_HEADS * HEAD_DIM
    d_in = w_in_ab.shape[2]
    d_in_pad = -(-d_in // (5 * LANES)) * (5 * LANES)
    alpha = (2.0 * depth) ** 0.25
    bf16 = jnp.bfloat16

    xp = x_prompt.reshape(n_seq * t, d)
    xs = x_sample.reshape(n_dec, d)
    n_phys = cache_k.shape[0]
    ck = cache_k.reshape(n_phys, -1, PAGE, d_att)
    cv = cache_v.reshape(n_phys, -1, PAGE, d_att)
    clf_t = jnp.swapaxes(cache_logf, 2, 3)

    kp_l, vp_l, fp_l, cp_l, ks_l, vs_l, fs_l, cs_l, sv_l = [], [], [], [], [], [], [], [], []
    for l in range(depth):
        i = l // 2
        if l % 2 == 0:
            w_in = jnp.pad(w_in_ab[i], ((0, 0), (0, d_in_pad - d_in))).astype(bf16)
            w_o = w_o_ab[i].astype(bf16)
            xw = _matmul(xp, w_in, d_in_pad // 5)
            a_out, lf, conv_new = _conv_prompt(xw, b_f[i], conv_w[i], n_seq, d_conv)
            cum_t = _cumsum_time(jnp.swapaxes(lf.reshape(n_seq, t, N_HEADS), 1, 2))
            cum = jnp.swapaxes(cum_t, 1, 2).reshape(n_seq * t, N_HEADS)
            o = _fox_prompt(xw, cum, cum_t, n_seq, d_conv)
            kp_l.append(xw[:, 3 * d_conv + d_att:3 * d_conv + 2 * d_att].reshape(n_seq, t, N_HEADS, HEAD_DIM))
            vp_l.append(xw[:, 3 * d_conv + 2 * d_att:3 * d_conv + 3 * d_att].reshape(n_seq, t, N_HEADS, HEAD_DIM))
            fp_l.append(lf.reshape(n_seq, t, N_HEADS))
            cp_l.append(conv_new)
            xp = _matmul_res_ln([a_out, o], [w_o[:d_conv], w_o[d_conv:]], xp, ln1_g[l], ln1_b[l], alpha)
            xw = _matmul(xs, w_in, d_in_pad // 5)
            a_out, lf, z = _conv_sample(xw, state_conv[:, i, 0], state_conv[:, i, 1], b_f[i], conv_w[i], d_conv)
            q = xw[:, 3 * d_conv:3 * d_conv + d_att]
            k = xw[:, 3 * d_conv + d_att:3 * d_conv + 2 * d_att]
            v = xw[:, 3 * d_conv + 2 * d_att:3 * d_conv + 3 * d_att]
            o = _fox_sample(q, k, v, lf, ck, cv, clf_t, page_table, i)
            ks_l.append(k.reshape(n_dec, 1, N_HEADS, HEAD_DIM))
            vs_l.append(v.reshape(n_dec, 1, N_HEADS, HEAD_DIM))
            fs_l.append(lf.reshape(n_dec, 1, N_HEADS))
            cs_l.append(jnp.stack([state_conv[:, i, 1], z], axis=1))
            xs = _matmul_res_ln([a_out, o], [w_o[:d_conv], w_o[d_conv:]], xs, ln1_g[l], ln1_b[l], alpha)
        else:
            w_in = w_in_c[i].astype(bf16)
            w_o = w_o_c[i].astype(bf16)
            d_sg = w_o.shape[0]
            hw = _matmul(xp, w_in, d_sg)
            sg = _sg_prompt(hw, sg_ln_g[i], sg_ln_b[i], w_s[i], b_s[i].T)
            xp = _matmul_res_ln([sg], [w_o], xp, ln1_g[l], ln1_b[l], alpha)
            hw = _matmul(xs, w_in, d_sg)
            gd = d_sg // SG_GROUPS
            sg, sv = _sg_sample(hw, sg_ln_g[i], sg_ln_b[i], jnp.repeat(w_s[i, :, 0, 0], gd), jnp.repeat(b_s[i, :, 0], gd))
            sv_l.append(sv.reshape(n_dec, 1, d_sg))
            xs = _matmul_res_ln([sg], [w_o], xs, ln1_g[l], ln1_b[l], alpha)
        wq = peer_wq[l].astype(bf16)
        uv = _pack_bf16_pair(peer_u[l], peer_v[l]).reshape(-1, SUBLANES, LANES)
        for which in range(2):
            xx = xp if which == 0 else xs
            qq = _matmul(xx, wq, wq.shape[1] // 2)
            experts, gates = _peer_topk(qq, peer_keys[l])
            xx = _peer_gather_ln(xx, experts, gates, uv, ln2_g[l], ln2_b[l], alpha)
            if which == 0:
                xp = xx
            else:
                xs = xx

    return (xp.reshape(n_seq, t, d), xs.reshape(n_dec, 1, d),
            jnp.stack(kp_l, axis=1), jnp.stack(vp_l, axis=1), jnp.stack(fp_l, axis=1),
            jnp.stack(ks_l, axis=1), jnp.stack(vs_l, axis=1), jnp.stack(fs_l, axis=1),
            jnp.stack(cp_l, axis=1), jnp.stack(cs_l, axis=1), jnp.stack(sv_l, axis=1))
```

```python
import functools
import math

import jax
import jax.numpy as jnp
from jax import lax
from jax.experimental import pallas as pl
from jax.experimental.pallas import tpu as pltpu

LANES = 128
SUBLANES = 8
VMEM_LIMIT = 48 * 1024 * 1024

LN_EPS = 1e-5
HEAD_DIM = 64
N_HEADS = 8
PAGE = 128
CHUNK = 128
SG_GROUPS = 8
PEER_HEADS = 8
PEER_NKEYS = 128
PEER_TOPK = 16
PEER_SLOTS = PEER_HEADS * PEER_TOPK
GATHER_GROUP = 8
DMA_THREADS = 2
GATHER_BLOCK = 64
TOKENS_PER_ITER = 2
DECODE_PAGES_PER_STEP = 8
NEG_INF = float("-inf")


def _params(*sem, **kw):
    return pltpu.CompilerParams(dimension_semantics=sem, vmem_limit_bytes=VMEM_LIMIT, **kw)


def _gelu(x):
    return 0.5 * x * (1.0 + jnp.tanh(math.sqrt(2.0 / math.pi) * (x + 0.044715 * (x * x * x))))


def _layer_norm(x, g, b):
    mu = jnp.mean(x, axis=-1, keepdims=True)
    xc = x - mu
    var = jnp.mean(xc * xc, axis=-1, keepdims=True)
    return xc * lax.rsqrt(var + LN_EPS) * g + b


def _log_sigmoid(x):
    return -(jnp.maximum(-x, 0.0) + jnp.log1p(jnp.exp(-jnp.abs(x))))


def _bdot(a, b):
    return jnp.dot(a.astype(jnp.bfloat16), b.astype(jnp.bfloat16), preferred_element_type=jnp.float32)


def _bdot_nt(a, b):
    return lax.dot_general(a.astype(jnp.bfloat16), b.astype(jnp.bfloat16), (((1,), (1,)), ((), ())),
                           preferred_element_type=jnp.float32)


def _row_block(n, pref):
    return pref if n % pref == 0 else n


def _mm_kernel(x_ref, w_ref, o_ref):
    o_ref[...] = _bdot(x_ref[...], w_ref[...])


def _matmul(x, w, tn):
    n, k = x.shape
    m = w.shape[1]
    tm = _row_block(n, 512)
    return pl.pallas_call(
        _mm_kernel,
        grid=(m // tn, n // tm),
        in_specs=[pl.BlockSpec((tm, k), lambda j, i: (i, 0)),
                  pl.BlockSpec((k, tn), lambda j, i: (0, j))],
        out_specs=pl.BlockSpec((tm, tn), lambda j, i: (i, j)),
        out_shape=jax.ShapeDtypeStruct((n, m), jnp.float32),
        compiler_params=_params("arbitrary", "arbitrary"),
        name="matmul",
    )(x, w)


def _mm_res_ln_kernel(*refs, n_in, alpha):
    xs, ws = refs[:n_in], refs[n_in:2 * n_in]
    res_ref, g_ref, b_ref, o_ref = refs[2 * n_in:]
    y = alpha * res_ref[...]
    for x_ref, w_ref in zip(xs, ws):
        y = y + _bdot(x_ref[...], w_ref[...])
    o_ref[...] = _layer_norm(y, g_ref[...], b_ref[...])


def _matmul_res_ln(xs, ws, res, g, b, alpha):
    n, d = res.shape
    tm = _row_block(n, 512)
    row = lambda i: (i, 0)
    fixed = lambda i: (0, 0)
    in_specs = ([pl.BlockSpec((tm, x.shape[1]), row) for x in xs]
                + [pl.BlockSpec(w.shape, fixed) for w in ws]
                + [pl.BlockSpec((tm, d), row), pl.BlockSpec((1, d), fixed), pl.BlockSpec((1, d), fixed)])
    return pl.pallas_call(
        functools.partial(_mm_res_ln_kernel, n_in=len(xs), alpha=alpha),
        grid=(n // tm,),
        in_specs=in_specs,
        out_specs=pl.BlockSpec((tm, d), row),
        out_shape=jax.ShapeDtypeStruct((n, d), jnp.float32),
        compiler_params=_params("arbitrary"),
        name="matmul_res_ln",
    )(*xs, *ws, res, g.reshape(1, d), b.reshape(1, d))


def _conv_prompt_kernel(gb_ref, gc_ref, hc_ref, gcp_ref, hcp_ref, fl_ref, bf_ref, cw_ref,
                        a_ref, lf_ref, cn_ref, zs_ref, *, tm, blocks_per_seq):
    i = pl.program_id(0)
    z = gc_ref[...] * hc_ref[...]
    zprev = gcp_ref[...] * hcp_ref[...]
    zprev = jnp.where(i % blocks_per_seq == 0, jnp.zeros_like(zprev), zprev)
    zs_ref[0:SUBLANES, :] = zprev
    zs_ref[SUBLANES:, :] = z
    z1 = zs_ref[pl.ds(SUBLANES - 1, tm), :]
    z2 = zs_ref[pl.ds(SUBLANES - 2, tm), :]
    cw = cw_ref[...]
    y = cw[0:1, :] * z2 + cw[1:2, :] * z1 + cw[2:3, :] * z
    a_ref[...] = gb_ref[...] * y
    lf_ref[...] = _log_sigmoid(fl_ref[:, 0:N_HEADS] + bf_ref[...])
    cn_ref[0] = zs_ref[pl.ds(tm + SUBLANES - 2, 2), :]


def _conv_prompt(xw, b_f, conv_w, n_seq, d_conv):
    n = xw.shape[0]
    t = n // n_seq
    tm = 512
    bps = t // tm
    fl_blk = (6 * d_conv) // LANES
    col = lambda c: (lambda i: (i, c))
    prev = lambda c: (lambda i: (jnp.maximum(i * (tm // SUBLANES) - 1, 0), c))
    return pl.pallas_call(
        functools.partial(_conv_prompt_kernel, tm=tm, blocks_per_seq=bps),
        grid=(n // tm,),
        in_specs=[pl.BlockSpec((tm, d_conv), col(0)), pl.BlockSpec((tm, d_conv), col(1)),
                  pl.BlockSpec((tm, d_conv), col(2)),
                  pl.BlockSpec((SUBLANES, d_conv), prev(1)), pl.BlockSpec((SUBLANES, d_conv), prev(2)),
                  pl.BlockSpec((tm, LANES), col(fl_blk)),
                  pl.BlockSpec((1, N_HEADS), lambda i: (0, 0)),
                  pl.BlockSpec(conv_w.shape, lambda i: (0, 0))],
        out_specs=[pl.BlockSpec((tm, d_conv), lambda i: (i, 0)),
                   pl.BlockSpec((tm, N_HEADS), lambda i: (i, 0)),
                   pl.BlockSpec((1, 2, d_conv), lambda i: (i // bps, 0, 0))],
        out_shape=[jax.ShapeDtypeStruct((n, d_conv), jnp.float32),
                   jax.ShapeDtypeStruct((n, N_HEADS), jnp.float32),
                   jax.ShapeDtypeStruct((n_seq, 2, d_conv), jnp.float32)],
        scratch_shapes=[pltpu.VMEM((tm + SUBLANES, d_conv), jnp.float32)],
        compiler_params=_params("arbitrary"),
        name="conv_prompt",
    )(xw, xw, xw, xw, xw, xw, b_f.reshape(1, N_HEADS), conv_w)


def _conv_sample_kernel(gb_ref, gc_ref, hc_ref, fl_ref, p0_ref, p1_ref, bf_ref, cw_ref, a_ref, lf_ref, z_ref):
    z = gc_ref[...] * hc_ref[...]
    cw = cw_ref[...]
    y = cw[0:1, :] * p0_ref[...] + cw[1:2, :] * p1_ref[...] + cw[2:3, :] * z
    a_ref[...] = gb_ref[...] * y
    lf_ref[...] = _log_sigmoid(fl_ref[:, 0:N_HEADS] + bf_ref[...])
    z_ref[...] = z


def _conv_sample(xw, prev0, prev1, b_f, conv_w, d_conv):
    n = xw.shape[0]
    fl_blk = (6 * d_conv) // LANES
    full = lambda shp: pl.BlockSpec(shp, lambda i: (0,) * len(shp))
    return pl.pallas_call(
        _conv_sample_kernel,
        grid=(1,),
        in_specs=[pl.BlockSpec((n, d_conv), lambda i: (0, 0)), pl.BlockSpec((n, d_conv), lambda i: (0, 1)),
                  pl.BlockSpec((n, d_conv), lambda i: (0, 2)), pl.BlockSpec((n, LANES), lambda i: (0, fl_blk)),
                  full((n, d_conv)), full((n, d_conv)), full((1, N_HEADS)), full(conv_w.shape)],
        out_specs=[full((n, d_conv)), full((n, N_HEADS)), full((n, d_conv))],
        out_shape=[jax.ShapeDtypeStruct((n, d_conv), jnp.float32),
                   jax.ShapeDtypeStruct((n, N_HEADS), jnp.float32),
                   jax.ShapeDtypeStruct((n, d_conv), jnp.float32)],
        compiler_params=_params("arbitrary"),
        name="conv_sample",
    )(xw, xw, xw, xw, prev0, prev1, b_f.reshape(1, N_HEADS), conv_w)


def _lane_prefix_sum(x):
    lane = lax.broadcasted_iota(jnp.int32, x.shape, 1)
    d = 1
    while d < LANES:
        x = x + jnp.where(lane >= d, pltpu.roll(x, d, 1), 0.0)
        d *= 2
    return x


def _lane_suffix_sum(x):
    lane = lax.broadcasted_iota(jnp.int32, x.shape, 1)
    d = 1
    while d < LANES:
        x = x + jnp.where(lane < LANES - d, pltpu.roll(x, LANES - d, 1), 0.0)
        d *= 2
    return x


def _cumsum_kernel(x_ref, o_ref, *, t):
    carry = jnp.zeros((N_HEADS, 1), jnp.float32)
    for c in range(t // LANES):
        sl = slice(c * LANES, (c + 1) * LANES)
        s = _lane_prefix_sum(x_ref[0, :, sl]) + carry
        o_ref[0, :, sl] = s
        carry = s[:, LANES - 1:LANES]


def _cumsum_time(lf_t):
    n_seq, h, t = lf_t.shape
    return pl.pallas_call(
        functools.partial(_cumsum_kernel, t=t),
        grid=(n_seq,),
        in_specs=[pl.BlockSpec((1, h, t), lambda b: (b, 0, 0))],
        out_specs=pl.BlockSpec((1, h, t), lambda b: (b, 0, 0)),
        out_shape=jax.ShapeDtypeStruct(lf_t.shape, jnp.float32),
        compiler_params=_params("arbitrary"),
        name="cumsum_time",
    )(lf_t)


def _fox_prompt_kernel(q_ref, k_ref, v_ref, cq_ref, ck_ref, o_ref, m_ref, l_ref, acc_ref, *, tq, scale):
    qb, kb = pl.program_id(1), pl.program_id(2)

    @pl.when(kb == 0)
    def _():
        m_ref[...] = jnp.full(m_ref.shape, NEG_INF, jnp.float32)
        l_ref[...] = jnp.zeros(l_ref.shape, jnp.float32)
        acc_ref[...] = jnp.zeros(acc_ref.shape, jnp.float32)

    @pl.when(kb <= qb)
    def _():
        row = qb * tq + lax.broadcasted_iota(jnp.int32, (tq, tq), 0)
        colp = kb * tq + lax.broadcasted_iota(jnp.int32, (tq, tq), 1)
        causal = colp <= row
        lane = lax.broadcasted_iota(jnp.int32, (1, LANES), 1)
        for h in range(N_HEADS):
            pair = slice((h // 2) * LANES, (h // 2 + 1) * LANES)
            own = (lane // HEAD_DIM) == (h % 2)
            qh = jnp.where(own, q_ref[:, pair], 0.0)
            s = _bdot_nt(qh, k_ref[:, pair]) * scale
            s = s + cq_ref[:, h:h + 1] - ck_ref[0, h:h + 1, :]
            s = jnp.where(causal, s, NEG_INF)
            m_old = m_ref[h]
            m_new = jnp.maximum(m_old, jnp.max(s, axis=1, keepdims=True))
            p = jnp.exp(s - m_new)
            a = jnp.exp(m_old - m_new)
            l_ref[h] = a * l_ref[h] + jnp.sum(p, axis=1, keepdims=True)
            acc_ref[h] = a * acc_ref[h] + _bdot(p, v_ref[:, pair])
            m_ref[h] = m_new

    @pl.when(kb == qb)
    def _():
        lane = lax.broadcasted_iota(jnp.int32, (1, LANES), 1)
        for hp in range(N_HEADS // 2):
            o0 = acc_ref[2 * hp] / l_ref[2 * hp]
            o1 = acc_ref[2 * hp + 1] / l_ref[2 * hp + 1]
            o_ref[:, hp * LANES:(hp + 1) * LANES] = jnp.where(lane < HEAD_DIM, o0, o1)


def _fox_prompt(xw, cum, cum_t, n_seq, d_conv):
    n = xw.shape[0]
    t = n // n_seq
    d_att = N_HEADS * HEAD_DIM
    tq = 512
    nb = t // tq
    qc, kc, vc = (3 * d_conv) // d_att, (3 * d_conv) // d_att + 1, (3 * d_conv) // d_att + 2
    return pl.pallas_call(
        functools.partial(_fox_prompt_kernel, tq=tq, scale=HEAD_DIM ** -0.5),
        grid=(n_seq, nb, nb),
        in_specs=[pl.BlockSpec((tq, d_att), lambda b, i, j: (b * nb + i, qc)),
                  pl.BlockSpec((tq, d_att), lambda b, i, j: (b * nb + jnp.minimum(i, j), kc)),
                  pl.BlockSpec((tq, d_att), lambda b, i, j: (b * nb + jnp.minimum(i, j), vc)),
                  pl.BlockSpec((tq, N_HEADS), lambda b, i, j: (b * nb + i, 0)),
                  pl.BlockSpec((1, N_HEADS, tq), lambda b, i, j: (b, 0, jnp.minimum(i, j)))],
        out_specs=pl.BlockSpec((tq, d_att), lambda b, i, j: (b * nb + i, 0)),
        out_shape=jax.ShapeDtypeStruct((n, d_att), jnp.float32),
        scratch_shapes=[pltpu.VMEM((N_HEADS, tq, 1), jnp.float32), pltpu.VMEM((N_HEADS, tq, 1), jnp.float32),
                        pltpu.VMEM((N_HEADS, tq, LANES), jnp.float32)],
        compiler_params=_params("arbitrary", "arbitrary", "arbitrary"),
        name="fox_prompt",
    )(xw, xw, xw, cum, cum_t)


def _fox_sample_kernel(pt_ref, q_ref, kn_ref, vn_ref, lfn_ref, *refs, scale, pp):
    ck_refs, cv_refs, clf_refs = refs[:pp], refs[pp:2 * pp], refs[2 * pp:3 * pp]
    o_ref, qm_ref, m_ref, l_ref, acc_ref, d_ref = refs[3 * pp:]
    j = pl.program_id(1)
    d_att = N_HEADS * HEAD_DIM
    head_of_lane = lax.broadcasted_iota(jnp.int32, (N_HEADS, d_att), 1) // HEAD_DIM
    head_of_row = lax.broadcasted_iota(jnp.int32, (N_HEADS, d_att), 0)
    own = head_of_lane == head_of_row

    @pl.when(j == 0)
    def _():
        qm = jnp.where(own, jnp.broadcast_to(q_ref[0], (N_HEADS, d_att)), 0.0)
        qm_ref[...] = qm
        m_ref[...] = jnp.sum(qm * kn_ref[0], axis=1, keepdims=True) * scale
        l_ref[...] = jnp.ones(l_ref.shape, jnp.float32)
        acc_ref[...] = jnp.broadcast_to(vn_ref[0], (N_HEADS, d_att))
        eye = (lax.broadcasted_iota(jnp.int32, (N_HEADS, N_HEADS), 0)
               == lax.broadcasted_iota(jnp.int32, (N_HEADS, N_HEADS), 1))
        d_ref[...] = jnp.sum(jnp.where(eye, jnp.broadcast_to(lfn_ref[0], (N_HEADS, N_HEADS)), 0.0),
                             axis=1, keepdims=True)

    decay = d_ref[...]
    qm = qm_ref[...]
    ss = []
    for r in range(pp):
        lf = clf_refs[r][0, 0]
        suf = _lane_suffix_sum(lf)
        ss.append(_bdot_nt(qm, ck_refs[r][0, 0]) * scale + (decay + (suf - lf)))
        decay = decay + suf[:, 0:1]
    d_ref[...] = decay
    m_old = m_ref[...]
    m_new = m_old
    for s in ss:
        m_new = jnp.maximum(m_new, jnp.max(s, axis=1, keepdims=True))
    a = jnp.exp(m_old - m_new)
    l_new = a * l_ref[...]
    acc = a * acc_ref[...]
    for r in range(pp):
        p = jnp.exp(ss[r] - m_new)
        l_new = l_new + jnp.sum(p, axis=1, keepdims=True)
        acc = acc + _bdot(p, cv_refs[r][0, 0])
    l_ref[...] = l_new
    acc_ref[...] = acc
    m_ref[...] = m_new

    @pl.when(j == pl.num_programs(1) - 1)
    def _():
        o = acc_ref[...] / l_ref[...]
        o_ref[0] = jnp.sum(jnp.where(own, o, 0.0), axis=0, keepdims=True)


def _fox_sample(q, k_new, v_new, lf_new, cache_k, cache_v, cache_lf_t, page_table, layer):
    n, d_att = q.shape
    n_pages = page_table.shape[1]
    pp = math.gcd(n_pages, DECODE_PAGES_PER_STEP)
    per_seq = lambda shp: pl.BlockSpec((1,) + shp, lambda b, j, pt: (b, 0, 0))

    def paged(shp, r):
        return pl.BlockSpec((1, 1) + shp, lambda b, j, pt: (pt[b, n_pages - 1 - (j * pp + r)], layer, 0, 0))

    grid_spec = pltpu.PrefetchScalarGridSpec(
        num_scalar_prefetch=1,
        grid=(n, n_pages // pp),
        in_specs=([per_seq((1, d_att)), per_seq((1, d_att)), per_seq((1, d_att)), per_seq((1, N_HEADS))]
                  + [paged((PAGE, d_att), r) for r in range(pp)] + [paged((PAGE, d_att), r) for r in range(pp)]
                  + [paged((N_HEADS, PAGE), r) for r in range(pp)]),
        out_specs=per_seq((1, d_att)),
        scratch_shapes=[pltpu.VMEM((N_HEADS, d_att), jnp.float32), pltpu.VMEM((N_HEADS, 1), jnp.float32),
                        pltpu.VMEM((N_HEADS, 1), jnp.float32), pltpu.VMEM((N_HEADS, d_att), jnp.float32),
                        pltpu.VMEM((N_HEADS, 1), jnp.float32)],
    )
    o = pl.pallas_call(
        functools.partial(_fox_sample_kernel, scale=HEAD_DIM ** -0.5, pp=pp),
        grid_spec=grid_spec,
        out_shape=jax.ShapeDtypeStruct((n, 1, d_att), jnp.float32),
        compiler_params=_params("arbitrary", "arbitrary"),
        name="fox_sample",
    )(page_table, q.reshape(n, 1, d_att), k_new.reshape(n, 1, d_att), v_new.reshape(n, 1, d_att),
      lf_new.reshape(n, 1, N_HEADS), *([cache_k] * pp), *([cache_v] * pp), *([cache_lf_t] * pp))
    return o.reshape(n, d_att)


def _sg_prompt_kernel(u_ref, v_ref, g_ref, b_ref, ws_ref, bs_ref, o_ref, *, tm):
    v = _layer_norm(_gelu(v_ref[...]), g_ref[...], b_ref[...])
    tri = (lax.broadcasted_iota(jnp.int32, (CHUNK, CHUNK), 1)
           <= lax.broadcasted_iota(jnp.int32, (CHUNK, CHUNK), 0))
    gd = v.shape[1] // SG_GROUPS
    for g in range(SG_GROUPS):
        w = jnp.where(tri, ws_ref[g], 0.0)
        bias = bs_ref[:, g:g + 1]
        cols = slice(g * gd, (g + 1) * gd)
        for c in range(tm // CHUNK):
            rows = slice(c * CHUNK, (c + 1) * CHUNK)
            y = _bdot(w, v[rows, cols]) + bias
            o_ref[rows, cols] = _gelu(u_ref[rows, cols]) * y


def _sg_prompt(hw, ln_g, ln_b, w_s, b_s_t):
    n, d2 = hw.shape
    d = d2 // 2
    tm = 512
    fixed = lambda shp: pl.BlockSpec(shp, lambda i: (0,) * len(shp))
    return pl.pallas_call(
        functools.partial(_sg_prompt_kernel, tm=tm),
        grid=(n // tm,),
        in_specs=[pl.BlockSpec((tm, d), lambda i: (i, 0)), pl.BlockSpec((tm, d), lambda i: (i, 1)),
                  fixed((1, d)), fixed((1, d)), fixed(w_s.shape), fixed(b_s_t.shape)],
        out_specs=pl.BlockSpec((tm, d), lambda i: (i, 0)),
        out_shape=jax.ShapeDtypeStruct((n, d), jnp.float32),
        compiler_params=_params("arbitrary"),
        name="sg_prompt",
    )(hw, hw, ln_g.reshape(1, d), ln_b.reshape(1, d), w_s, b_s_t)


def _sg_sample_kernel(u_ref, v_ref, g_ref, b_ref, wd_ref, b0_ref, o_ref, vo_ref):
    v = _layer_norm(_gelu(v_ref[...]), g_ref[...], b_ref[...])
    vo_ref[...] = v
    o_ref[...] = _gelu(u_ref[...]) * (wd_ref[...] * v + b0_ref[...])


def _sg_sample(hw, ln_g, ln_b, w_diag0, b0):
    n, d2 = hw.shape
    d = d2 // 2
    fixed = lambda shp: pl.BlockSpec(shp, lambda i: (0,) * len(shp))
    return pl.pallas_call(
        _sg_sample_kernel,
        grid=(1,),
        in_specs=[pl.BlockSpec((n, d), lambda i: (0, 0)), pl.BlockSpec((n, d), lambda i: (0, 1)),
                  fixed((1, d)), fixed((1, d)), fixed((1, d)), fixed((1, d))],
        out_specs=[fixed((n, d)), fixed((n, d))],
        out_shape=[jax.ShapeDtypeStruct((n, d), jnp.float32), jax.ShapeDtypeStruct((n, d), jnp.float32)],
        compiler_params=_params("arbitrary"),
        name="sg_sample",
    )(hw, hw, ln_g.reshape(1, d), ln_b.reshape(1, d), w_diag0.reshape(1, d), b0.reshape(1, d))


def _top16(s, pos, ids):
    far = jnp.iinfo(jnp.int32).max
    vals, outs = [], []
    for _ in range(PEER_TOPK):
        m = jnp.max(s, axis=0, keepdims=True)
        am = jnp.min(jnp.where(s == m, pos, far), axis=0, keepdims=True)
        hit = pos == am
        vals.append(m)
        outs.append(am if ids is None else jnp.max(jnp.where(hit, ids, -1), axis=0, keepdims=True))
        s = jnp.where(hit, NEG_INF, s)
    return jnp.concatenate(vals, axis=0), jnp.concatenate(outs, axis=0)


def _peer_topk_kernel(q_ref, keys_ref, e_ref, g_ref):
    tb = q_ref.shape[0]
    key_pos = lax.broadcasted_iota(jnp.int32, (PEER_NKEYS, tb), 0)
    half = []
    for p in range(2):
        s = _bdot_nt(keys_ref[0, p], q_ref[:, p * PEER_NKEYS:(p + 1) * PEER_NKEYS])
        half.append(_top16(s, key_pos, None))
    (s0, i0), (s1, i1) = half
    row8 = lax.broadcasted_iota(jnp.int32, (SUBLANES, tb), 0)
    row16 = lax.broadcasted_iota(jnp.int32, (PEER_TOPK, tb), 0)
    cs, ci, cp = [s0[0:1, :] + s1], [i0[0:1, :] * PEER_NKEYS + i1], [row16]
    for a in range(1, SUBLANES):
        keep = row8 < PEER_TOPK // (a + 1)
        cs.append(jnp.where(keep, s0[a:a + 1, :] + s1[0:SUBLANES, :], NEG_INF))
        ci.append(i0[a:a + 1, :] * PEER_NKEYS + i1[0:SUBLANES, :])
        cp.append(jnp.where(keep, a * PEER_TOPK + row8, PEER_TOPK * PEER_TOPK + a * PEER_TOPK + row8))
    cs.append(s0[SUBLANES:, :] + s1[0:1, :])
    ci.append(i0[SUBLANES:, :] * PEER_NKEYS + i1[0:1, :])
    cp.append((row8 + SUBLANES) * PEER_TOPK)
    best_s, experts = _top16(jnp.concatenate(cs, axis=0), jnp.concatenate(cp, axis=0), jnp.concatenate(ci, axis=0))
    e = jnp.exp(best_s - best_s[0:1, :])
    e_ref[0] = experts
    g_ref[0] = e / jnp.sum(e, axis=0, keepdims=True)


def _peer_topk(q, keys):
    n = q.shape[0]
    tb = _row_block(n, 256)
    dk = 2 * PEER_NKEYS
    out = pl.BlockSpec((1, PEER_TOPK, tb), lambda i, h: (h, 0, i))
    e, g = pl.pallas_call(
        _peer_topk_kernel,
        grid=(n // tb, PEER_HEADS),
        in_specs=[pl.BlockSpec((tb, dk), lambda i, h: (i, h)),
                  pl.BlockSpec((1, 2, PEER_NKEYS, dk // 2), lambda i, h: (h, 0, 0, 0))],
        out_specs=[out, out],
        out_shape=[jax.ShapeDtypeStruct((PEER_HEADS, PEER_TOPK, n), jnp.int32),
                   jax.ShapeDtypeStruct((PEER_HEADS, PEER_TOPK, n), jnp.float32)],
        compiler_params=_params("arbitrary", "arbitrary"),
        name="peer_topk",
    )(q, keys)
    return e.reshape(PEER_SLOTS, n), g.reshape(PEER_SLOTS, n)


def _tree_sum(xs):
    while len(xs) > 1:
        xs = [xs[k] + xs[k + 1] for k in range(0, len(xs) - 1, 2)] + ([xs[-1]] if len(xs) % 2 else [])
    return xs[0]


def _tile_mean(y):
    s = jnp.sum(jnp.sum(y, axis=-1, keepdims=True), axis=-2, keepdims=True)
    return s * (1.0 / (y.shape[-1] * y.shape[-2]))


def _fold_sublanes(a, b, r):
    low = (lax.broadcasted_iota(jnp.int32, a.shape, 0) % (2 * r)) < r
    return jnp.where(low, a, pltpu.roll(b, r, 0)) + jnp.where(low, pltpu.roll(a, SUBLANES - r, 0), b)


_FOLD_ORDER = (0, 4, 2, 6, 1, 5, 3, 7)


def _sublane_sums(tiles):
    v = [tiles[_FOLD_ORDER.index(k)] for k in range(SUBLANES)]
    r = SUBLANES // 2
    while len(v) > 1:
        v = [_fold_sublanes(v[2 * k], v[2 * k + 1], r) for k in range(len(v) // 2)]
        r //= 2
    return v[0]


def _peer_gather_kernel(idx_ref, x_ref, g_ref, lg_ref, lb_ref, uv_ref, o_ref, buf0_ref, buf1_ref, sem_ref,
                        *, tb, d, alpha):
    i = pl.program_id(0)
    n_groups = tb // GATHER_GROUP
    group_picks = GATHER_GROUP * PEER_SLOTS
    n_lane_tiles = d // LANES
    bufs = (buf0_ref, buf1_ref)

    def start(k, slot, t, row, thread=0):
        pltpu.make_async_copy(uv_ref.at[idx_ref[k]], bufs[slot].at[row], sem_ref.at[slot, t]).start(priority=thread)

    def wait(slot, t):
        rows = bufs[slot].at[pl.ds(t * PEER_SLOTS, PEER_SLOTS)]
        pltpu.make_async_copy(rows, rows, sem_ref.at[slot, t]).wait()

    def mix_token(tok, t, slot, pick0, slot_next):
        buf = bufs[slot]
        x = x_ref[tok]
        lane = lax.broadcasted_iota(jnp.int32, (SUBLANES, tb), 1)
        ys = []
        for r in range(PEER_SLOTS // SUBLANES):
            row0 = t * PEER_SLOTS + r * SUBLANES
            for jj in range(SUBLANES):
                start(pick0 + (row0 + jj), slot_next, t, row0 + jj, thread=jj % DMA_THREADS)
            words = [buf[row0 + jj] for jj in range(SUBLANES)]
            us = [lax.bitcast_convert_type(wd << 16, jnp.float32) for wd in words]
            vs = [lax.bitcast_convert_type(wd & jnp.uint32(0xFFFF0000), jnp.float32) for wd in words]
            h = jnp.sum(_sublane_sums([u * x for u in us]), axis=1, keepdims=True)
            gate = jnp.sum(jnp.where(lane == tok, g_ref[0, r * SUBLANES:(r + 1) * SUBLANES, :], 0.0),
                           axis=1, keepdims=True)
            w = jnp.broadcast_to(gate * _gelu(h), (SUBLANES, LANES))
            ys.append(_tree_sum([jnp.broadcast_to(w[jj:jj + 1, :], (SUBLANES, LANES)) * vs[jj]
                                 for jj in range(SUBLANES)]))
        o_ref[tok] = alpha * x + _tree_sum(ys)

    def mix_group(grp, slot, slot_next):
        def body(tt, c):
            for dt in range(TOKENS_PER_ITER):
                wait(slot, tt * TOKENS_PER_ITER + dt)
            for dt in range(TOKENS_PER_ITER):
                t = tt * TOKENS_PER_ITER + dt
                mix_token(grp * GATHER_GROUP + t, t, slot, (grp + 1) * group_picks, slot_next)
            return c
        lax.fori_loop(0, GATHER_GROUP // TOKENS_PER_ITER, body, 0)

    @pl.when(i == 0)
    def _():
        def body(k, c):
            start(k, 0, k // PEER_SLOTS, k)
            return c
        lax.fori_loop(0, group_picks, body, 0, unroll=8)

    def pair(gg, c):
        mix_group(2 * gg, 0, 1)
        mix_group(2 * gg + 1, 1, 0)
        return c
    lax.fori_loop(0, n_groups // 2, pair, 0)

    @pl.when(i == pl.num_programs(0) - 1)
    def _():
        for t in range(GATHER_GROUP):
            wait(0, t)

    y = o_ref[...]
    yc = y - _tile_mean(y)
    o_ref[...] = yc * lax.rsqrt(_tile_mean(yc * yc) + LN_EPS) * lg_ref[...] + lb_ref[...]


def _pack_bf16_pair(lo, hi):
    bits = lambda a: lax.bitcast_convert_type(a.astype(jnp.bfloat16), jnp.uint16).astype(jnp.uint32)
    return bits(lo) | (bits(hi) << 16)


def _peer_gather_ln(x, experts, gates, uv, ln_g, ln_b, alpha):
    n, d = x.shape
    tb = _row_block(n, GATHER_BLOCK)
    nblk = n // tb
    assert (tb // GATHER_GROUP) % 2 == 0 and d == SUBLANES * LANES
    group_picks = GATHER_GROUP * PEER_SLOTS
    picks = experts.T.reshape(nblk, tb * PEER_SLOTS)
    picks = jnp.concatenate([picks, jnp.roll(picks[:, :group_picks], -1, axis=0)], axis=1).reshape(-1)
    buf = pltpu.VMEM((group_picks, SUBLANES, LANES), jnp.uint32)
    tile = (SUBLANES, LANES)
    out = pl.pallas_call(
        functools.partial(_peer_gather_kernel, tb=tb, d=d, alpha=alpha),
        grid=(nblk,),
        in_specs=[pl.BlockSpec((tb * PEER_SLOTS + group_picks,), lambda i: (i,), memory_space=pltpu.SMEM),
                  pl.BlockSpec((tb,) + tile, lambda i: (i, 0, 0)),
                  pl.BlockSpec((1, PEER_SLOTS, tb), lambda i: (i, 0, 0)),
                  pl.BlockSpec(tile, lambda i: (0, 0)), pl.BlockSpec(tile, lambda i: (0, 0)),
                  pl.BlockSpec(memory_space=pl.ANY)],
        out_specs=pl.BlockSpec((tb,) + tile, lambda i: (i, 0, 0)),
        out_shape=jax.ShapeDtypeStruct((n,) + tile, jnp.float32),
        scratch_shapes=[buf, buf, pltpu.SemaphoreType.DMA((2, GATHER_GROUP))],
        compiler_params=_params("arbitrary", disable_bounds_checks=True),
        name="peer_gather",
    )(picks, x.reshape((n,) + tile), jnp.swapaxes(gates.reshape(PEER_SLOTS, nblk, tb), 0, 1),
      ln_g.reshape(tile), ln_b.reshape(tile), uv)
    return out.reshape(n, d)


def kernel(x_prompt, x_sample, cache_k, cache_v, cache_logf, state_conv, page_table, w_in_ab, b_f, conv_w, w_o_ab,
           w_in_c, sg_ln_g, sg_ln_b, w_s, b_s, w_o_c, ln1_g, ln1_b, ln2_g, ln2_b, peer_wq, peer_keys, peer_u, peer_v):
    n_seq, t, d = x_prompt.shape
    n_dec = x_sample.shape[0]
    depth = ln1_g.shape[0]
    d_conv = conv_w.shape[2]
    d_att = N_HEADS * HEAD_DIM
    d_in = w_in_ab.shape[2]
    d_in_pad = -(-d_in // (5 * LANES)) * (5 * LANES)
    alpha = (2.0 * depth) ** 0.25
    bf16 = jnp.bfloat16

    xp = x_prompt.reshape(n_seq * t, d)
    xs = x_sample.reshape(n_dec, d)
    n_phys = cache_k.shape[0]
    ck = cache_k.reshape(n_phys, -1, PAGE, d_att)
    cv = cache_v.reshape(n_phys, -1, PAGE, d_att)
    clf_t = jnp.swapaxes(cache_logf, 2, 3)

    kp_l, vp_l, fp_l, cp_l, ks_l, vs_l, fs_l, cs_l, sv_l = [], [], [], [], [], [], [], [], []
    for l in range(depth):
        i = l // 2
        if l % 2 == 0:
            w_in = jnp.pad(w_in_ab[i], ((0, 0), (0, d_in_pad - d_in))).astype(bf16)
            w_o = w_o_ab[i].astype(bf16)
            xw = _matmul(xp, w_in, d_in_pad // 5)
            a_out, lf, conv_new = _conv_prompt(xw, b_f[i], conv_w[i], n_seq, d_conv)
            cum_t = _cumsum_time(jnp.swapaxes(lf.reshape(n_seq, t, N_HEADS), 1, 2))
            cum = jnp.swapaxes(cum_t, 1, 2).reshape(n_seq * t, N_HEADS)
            o = _fox_prompt(xw, cum, cum_t, n_seq, d_conv)
            kp_l.append(xw[:, 3 * d_conv + d_att:3 * d_conv + 2 * d_att].reshape(n_seq, t, N_HEADS, HEAD_DIM))
            vp_l.append(xw[:, 3 * d_conv + 2 * d_att:3 * d_conv + 3 * d_att].reshape(n_seq, t, N_HEADS, HEAD_DIM))
            fp_l.append(lf.reshape(n_seq, t, N_HEADS))
            cp_l.append(conv_new)
            xp = _matmul_res_ln([a_out, o], [w_o[:d_conv], w_o[d_conv:]], xp, ln1_g[l], ln1_b[l], alpha)
            xw = _matmul(xs, w_in, d_in_pad // 5)
            a_out, lf, z = _conv_sample(xw, state_conv[:, i, 0], state_conv[:, i, 1], b_f[i], conv_w[i], d_conv)
            q = xw[:, 3 * d_conv:3 * d_conv + d_att]
            k = xw[:, 3 * d_conv + d_att:3 * d_conv + 2 * d_att]
            v = xw[:, 3 * d_conv + 2 * d_att:3 * d_conv + 3 * d_att]
            o = _fox_sample(q, k, v, lf, ck, cv, clf_t, page_table, i)
            ks_l.append(k.reshape(n_dec, 1, N_HEADS, HEAD_DIM))
            vs_l.append(v.reshape(n_dec, 1, N_HEADS, HEAD_DIM))
            fs_l.append(lf.reshape(n_dec, 1, N_HEADS))
            cs_l.append(jnp.stack([state_conv[:, i, 1], z], axis=1))
            xs = _matmul_res_ln([a_out, o], [w_o[:d_conv], w_o[d_conv:]], xs, ln1_g[l], ln1_b[l], alpha)
        else:
            w_in = w_in_c[i].astype(bf16)
            w_o = w_o_c[i].astype(bf16)
            d_sg = w_o.shape[0]
            hw = _matmul(xp, w_in, d_sg)
            sg = _sg_prompt(hw, sg_ln_g[i], sg_ln_b[i], w_s[i], b_s[i].T)
            xp = _matmul_res_ln([sg], [w_o], xp, ln1_g[l], ln1_b[l], alpha)
            hw = _matmul(xs, w_in, d_sg)
            gd = d_sg // SG_GROUPS
            sg, sv = _sg_sample(hw, sg_ln_g[i], sg_ln_b[i], jnp.repeat(w_s[i, :, 0, 0], gd), jnp.repeat(b_s[i, :, 0], gd))
            sv_l.append(sv.reshape(n_dec, 1, d_sg))
            xs = _matmul_res_ln([sg], [w_o], xs, ln1_g[l], ln1_b[l], alpha)
        wq = peer_wq[l].astype(bf16)
        uv = _pack_bf16_pair(peer_u[l], peer_v[l]).reshape(-1, SUBLANES, LANES)
        for which in range(2):
            xx = xp if which == 0 else xs
            qq = _matmul(xx, wq, wq.shape[1] // 2)
            experts, gates = _peer_topk(qq, peer_keys[l])
            xx = _peer_gather_ln(xx, experts, gates, uv, ln2_g[l], ln2_b[l], alpha)
            if which == 0:
                xp = xx
            else:
                xs = xx

    return (xp.reshape(n_seq, t, d), xs.reshape(n_dec, 1, d),
            jnp.stack(kp_l, axis=1), jnp.stack(vp_l, axis=1), jnp.stack(fp_l, axis=1),
            jnp.stack(ks_l, axis=1), jnp.stack(vs_l, axis=1), jnp.stack(fs_l, axis=1),
            jnp.stack(cp_l, axis=1), jnp.stack(cs_l, axis=1), jnp.stack(sv_l, axis=1))
```

```python
import functools
import math

import jax
import jax.numpy as jnp
from jax import lax
from jax.experimental import pallas as pl
from jax.experimental.pallas import tpu as pltpu

LANES = 128
SUBLANES = 8
VMEM_LIMIT = 48 * 1024 * 1024

LN_EPS = 1e-5
HEAD_DIM = 64
N_HEADS = 8
PAGE = 128
CHUNK = 128
SG_GROUPS = 8
PEER_HEADS = 8
PEER_NKEYS = 128
PEER_TOPK = 16
PEER_SLOTS = PEER_HEADS * PEER_TOPK
GATHER_GROUP = 8
DMA_THREADS = 2
GATHER_BLOCK = 64
TOKENS_PER_ITER = 4
DECODE_PAGES_PER_STEP = 8
NEG_INF = float("-inf")


def _params(*sem, **kw):
    return pltpu.CompilerParams(dimension_semantics=sem, vmem_limit_bytes=VMEM_LIMIT, **kw)


def _gelu(x):
    return 0.5 * x * (1.0 + jnp.tanh(math.sqrt(2.0 / math.pi) * (x + 0.044715 * (x * x * x))))


def _layer_norm(x, g, b):
    mu = jnp.mean(x, axis=-1, keepdims=True)
    xc = x - mu
    var = jnp.mean(xc * xc, axis=-1, keepdims=True)
    return xc * lax.rsqrt(var + LN_EPS) * g + b


def _log_sigmoid(x):
    return -(jnp.maximum(-x, 0.0) + jnp.log1p(jnp.exp(-jnp.abs(x))))


def _bdot(a, b):
    return jnp.dot(a.astype(jnp.bfloat16), b.astype(jnp.bfloat16), preferred_element_type=jnp.float32)


def _bdot_nt(a, b):
    return lax.dot_general(a.astype(jnp.bfloat16), b.astype(jnp.bfloat16), (((1,), (1,)), ((), ())),
                           preferred_element_type=jnp.float32)


def _row_block(n, pref):
    return pref if n % pref == 0 else n


def _mm_kernel(x_ref, w_ref, o_ref):
    o_ref[...] = _bdot(x_ref[...], w_ref[...])


def _matmul(x, w, tn):
    n, k = x.shape
    m = w.shape[1]
    tm = _row_block(n, 512)
    return pl.pallas_call(
        _mm_kernel,
        grid=(m // tn, n // tm),
        in_specs=[pl.BlockSpec((tm, k), lambda j, i: (i, 0)),
                  pl.BlockSpec((k, tn), lambda j, i: (0, j))],
        out_specs=pl.BlockSpec((tm, tn), lambda j, i: (i, j)),
        out_shape=jax.ShapeDtypeStruct((n, m), jnp.float32),
        compiler_params=_params("arbitrary", "arbitrary"),
        name="matmul",
    )(x, w)


def _mm_res_ln_kernel(*refs, n_in, alpha):
    xs, ws = refs[:n_in], refs[n_in:2 * n_in]
    res_ref, g_ref, b_ref, o_ref = refs[2 * n_in:]
    y = alpha * res_ref[...]
    for x_ref, w_ref in zip(xs, ws):
        y = y + _bdot(x_ref[...], w_ref[...])
    o_ref[...] = _layer_norm(y, g_ref[...], b_ref[...])


def _matmul_res_ln(xs, ws, res, g, b, alpha):
    n, d = res.shape
    tm = _row_block(n, 512)
    row = lambda i: (i, 0)
    fixed = lambda i: (0, 0)
    in_specs = ([pl.BlockSpec((tm, x.shape[1]), row) for x in xs]
                + [pl.BlockSpec(w.shape, fixed) for w in ws]
                + [pl.BlockSpec((tm, d), row), pl.BlockSpec((1, d), fixed), pl.BlockSpec((1, d), fixed)])
    return pl.pallas_call(
        functools.partial(_mm_res_ln_kernel, n_in=len(xs), alpha=alpha),
        grid=(n // tm,),
        in_specs=in_specs,
        out_specs=pl.BlockSpec((tm, d), row),
        out_shape=jax.ShapeDtypeStruct((n, d), jnp.float32),
        compiler_params=_params("arbitrary"),
        name="matmul_res_ln",
    )(*xs, *ws, res, g.reshape(1, d), b.reshape(1, d))


def _conv_prompt_kernel(gb_ref, gc_ref, hc_ref, gcp_ref, hcp_ref, fl_ref, bf_ref, cw_ref,
                        a_ref, lf_ref, cn_ref, zs_ref, *, tm, blocks_per_seq):
    i = pl.program_id(0)
    z = gc_ref[...] * hc_ref[...]
    zprev = gcp_ref[...] * hcp_ref[...]
    zprev = jnp.where(i % blocks_per_seq == 0, jnp.zeros_like(zprev), zprev)
    zs_ref[0:SUBLANES, :] = zprev
    zs_ref[SUBLANES:, :] = z
    z1 = zs_ref[pl.ds(SUBLANES - 1, tm), :]
    z2 = zs_ref[pl.ds(SUBLANES - 2, tm), :]
    cw = cw_ref[...]
    y = cw[0:1, :] * z2 + cw[1:2, :] * z1 + cw[2:3, :] * z
    a_ref[...] = gb_ref[...] * y
    lf_ref[...] = _log_sigmoid(fl_ref[:, 0:N_HEADS] + bf_ref[...])
    cn_ref[0] = zs_ref[pl.ds(tm + SUBLANES - 2, 2), :]


def _conv_prompt(xw, b_f, conv_w, n_seq, d_conv):
    n = xw.shape[0]
    t = n // n_seq
    tm = 512
    bps = t // tm
    fl_blk = (6 * d_conv) // LANES
    col = lambda c: (lambda i: (i, c))
    prev = lambda c: (lambda i: (jnp.maximum(i * (tm // SUBLANES) - 1, 0), c))
    return pl.pallas_call(
        functools.partial(_conv_prompt_kernel, tm=tm, blocks_per_seq=bps),
        grid=(n // tm,),
        in_specs=[pl.BlockSpec((tm, d_conv), col(0)), pl.BlockSpec((tm, d_conv), col(1)),
                  pl.BlockSpec((tm, d_conv), col(2)),
                  pl.BlockSpec((SUBLANES, d_conv), prev(1)), pl.BlockSpec((SUBLANES, d_conv), prev(2)),
                  pl.BlockSpec((tm, LANES), col(fl_blk)),
                  pl.BlockSpec((1, N_HEADS), lambda i: (0, 0)),
                  pl.BlockSpec(conv_w.shape, lambda i: (0, 0))],
        out_specs=[pl.BlockSpec((tm, d_conv), lambda i: (i, 0)),
                   pl.BlockSpec((tm, N_HEADS), lambda i: (i, 0)),
                   pl.BlockSpec((1, 2, d_conv), lambda i: (i // bps, 0, 0))],
        out_shape=[jax.ShapeDtypeStruct((n, d_conv), jnp.float32),
                   jax.ShapeDtypeStruct((n, N_HEADS), jnp.float32),
                   jax.ShapeDtypeStruct((n_seq, 2, d_conv), jnp.float32)],
        scratch_shapes=[pltpu.VMEM((tm + SUBLANES, d_conv), jnp.float32)],
        compiler_params=_params("arbitrary"),
        name="conv_prompt",
    )(xw, xw, xw, xw, xw, xw, b_f.reshape(1, N_HEADS), conv_w)


def _conv_sample_kernel(gb_ref, gc_ref, hc_ref, fl_ref, p0_ref, p1_ref, bf_ref, cw_ref, a_ref, lf_ref, z_ref):
    z = gc_ref[...] * hc_ref[...]
    cw = cw_ref[...]
    y = cw[0:1, :] * p0_ref[...] + cw[1:2, :] * p1_ref[...] + cw[2:3, :] * z
    a_ref[...] = gb_ref[...] * y
    lf_ref[...] = _log_sigmoid(fl_ref[:, 0:N_HEADS] + bf_ref[...])
    z_ref[...] = z


def _conv_sample(xw, prev0, prev1, b_f, conv_w, d_conv):
    n = xw.shape[0]
    fl_blk = (6 * d_conv) // LANES
    full = lambda shp: pl.BlockSpec(shp, lambda i: (0,) * len(shp))
    return pl.pallas_call(
        _conv_sample_kernel,
        grid=(1,),
        in_specs=[pl.BlockSpec((n, d_conv), lambda i: (0, 0)), pl.BlockSpec((n, d_conv), lambda i: (0, 1)),
                  pl.BlockSpec((n, d_conv), lambda i: (0, 2)), pl.BlockSpec((n, LANES), lambda i: (0, fl_blk)),
                  full((n, d_conv)), full((n, d_conv)), full((1, N_HEADS)), full(conv_w.shape)],
        out_specs=[full((n, d_conv)), full((n, N_HEADS)), full((n, d_conv))],
        out_shape=[jax.ShapeDtypeStruct((n, d_conv), jnp.float32),
                   jax.ShapeDtypeStruct((n, N_HEADS), jnp.float32),
                   jax.ShapeDtypeStruct((n, d_conv), jnp.float32)],
        compiler_params=_params("arbitrary"),
        name="conv_sample",
    )(xw, xw, xw, xw, prev0, prev1, b_f.reshape(1, N_HEADS), conv_w)


def _lane_prefix_sum(x):
    lane = lax.broadcasted_iota(jnp.int32, x.shape, 1)
    d = 1
    while d < LANES:
        x = x + jnp.where(lane >= d, pltpu.roll(x, d, 1), 0.0)
        d *= 2
    return x


def _lane_suffix_sum(x):
    lane = lax.broadcasted_iota(jnp.int32, x.shape, 1)
    d = 1
    while d < LANES:
        x = x + jnp.where(lane < LANES - d, pltpu.roll(x, LANES - d, 1), 0.0)
        d *= 2
    return x


def _cumsum_kernel(x_ref, o_ref, *, t):
    carry = jnp.zeros((N_HEADS, 1), jnp.float32)
    for c in range(t // LANES):
        sl = slice(c * LANES, (c + 1) * LANES)
        s = _lane_prefix_sum(x_ref[0, :, sl]) + carry
        o_ref[0, :, sl] = s
        carry = s[:, LANES - 1:LANES]


def _cumsum_time(lf_t):
    n_seq, h, t = lf_t.shape
    return pl.pallas_call(
        functools.partial(_cumsum_kernel, t=t),
        grid=(n_seq,),
        in_specs=[pl.BlockSpec((1, h, t), lambda b: (b, 0, 0))],
        out_specs=pl.BlockSpec((1, h, t), lambda b: (b, 0, 0)),
        out_shape=jax.ShapeDtypeStruct(lf_t.shape, jnp.float32),
        compiler_params=_params("arbitrary"),
        name="cumsum_time",
    )(lf_t)


def _fox_prompt_kernel(q_ref, k_ref, v_ref, cq_ref, ck_ref, o_ref, m_ref, l_ref, acc_ref, *, tq, scale):
    qb, kb = pl.program_id(1), pl.program_id(2)

    @pl.when(kb == 0)
    def _():
        m_ref[...] = jnp.full(m_ref.shape, NEG_INF, jnp.float32)
        l_ref[...] = jnp.zeros(l_ref.shape, jnp.float32)
        acc_ref[...] = jnp.zeros(acc_ref.shape, jnp.float32)

    @pl.when(kb <= qb)
    def _():
        row = qb * tq + lax.broadcasted_iota(jnp.int32, (tq, tq), 0)
        colp = kb * tq + lax.broadcasted_iota(jnp.int32, (tq, tq), 1)
        causal = colp <= row
        lane = lax.broadcasted_iota(jnp.int32, (1, LANES), 1)
        for h in range(N_HEADS):
            pair = slice((h // 2) * LANES, (h // 2 + 1) * LANES)
            own = (lane // HEAD_DIM) == (h % 2)
            qh = jnp.where(own, q_ref[:, pair], 0.0)
            s = _bdot_nt(qh, k_ref[:, pair]) * scale
            s = s + cq_ref[:, h:h + 1] - ck_ref[0, h:h + 1, :]
            s = jnp.where(causal, s, NEG_INF)
            m_old = m_ref[h]
            m_new = jnp.maximum(m_old, jnp.max(s, axis=1, keepdims=True))
            p = jnp.exp(s - m_new)
            a = jnp.exp(m_old - m_new)
            l_ref[h] = a * l_ref[h] + jnp.sum(p, axis=1, keepdims=True)
            acc_ref[h] = a * acc_ref[h] + _bdot(p, v_ref[:, pair])
            m_ref[h] = m_new

    @pl.when(kb == qb)
    def _():
        lane = lax.broadcasted_iota(jnp.int32, (1, LANES), 1)
        for hp in range(N_HEADS // 2):
            o0 = acc_ref[2 * hp] / l_ref[2 * hp]
            o1 = acc_ref[2 * hp + 1] / l_ref[2 * hp + 1]
            o_ref[:, hp * LANES:(hp + 1) * LANES] = jnp.where(lane < HEAD_DIM, o0, o1)


def _fox_prompt(xw, cum, cum_t, n_seq, d_conv):
    n = xw.shape[0]
    t = n // n_seq
    d_att = N_HEADS * HEAD_DIM
    tq = 512
    nb = t // tq
    qc, kc, vc = (3 * d_conv) // d_att, (3 * d_conv) // d_att + 1, (3 * d_conv) // d_att + 2
    return pl.pallas_call(
        functools.partial(_fox_prompt_kernel, tq=tq, scale=HEAD_DIM ** -0.5),
        grid=(n_seq, nb, nb),
        in_specs=[pl.BlockSpec((tq, d_att), lambda b, i, j: (b * nb + i, qc)),
                  pl.BlockSpec((tq, d_att), lambda b, i, j: (b * nb + jnp.minimum(i, j), kc)),
                  pl.BlockSpec((tq, d_att), lambda b, i, j: (b * nb + jnp.minimum(i, j), vc)),
                  pl.BlockSpec((tq, N_HEADS), lambda b, i, j: (b * nb + i, 0)),
                  pl.BlockSpec((1, N_HEADS, tq), lambda b, i, j: (b, 0, jnp.minimum(i, j)))],
        out_specs=pl.BlockSpec((tq, d_att), lambda b, i, j: (b * nb + i, 0)),
        out_shape=jax.ShapeDtypeStruct((n, d_att), jnp.float32),
        scratch_shapes=[pltpu.VMEM((N_HEADS, tq, 1), jnp.float32), pltpu.VMEM((N_HEADS, tq, 1), jnp.float32),
                        pltpu.VMEM((N_HEADS, tq, LANES), jnp.float32)],
        compiler_params=_params("arbitrary", "arbitrary", "arbitrary"),
        name="fox_prompt",
    )(xw, xw, xw, cum, cum_t)


def _fox_sample_kernel(pt_ref, q_ref, kn_ref, vn_ref, lfn_ref, *refs, scale, pp):
    ck_refs, cv_refs, clf_refs = refs[:pp], refs[pp:2 * pp], refs[2 * pp:3 * pp]
    o_ref, qm_ref, m_ref, l_ref, acc_ref, d_ref = refs[3 * pp:]
    j = pl.program_id(1)
    d_att = N_HEADS * HEAD_DIM
    head_of_lane = lax.broadcasted_iota(jnp.int32, (N_HEADS, d_att), 1) // HEAD_DIM
    head_of_row = lax.broadcasted_iota(jnp.int32, (N_HEADS, d_att), 0)
    own = head_of_lane == head_of_row

    @pl.when(j == 0)
    def _():
        qm = jnp.where(own, jnp.broadcast_to(q_ref[0], (N_HEADS, d_att)), 0.0)
        qm_ref[...] = qm
        m_ref[...] = jnp.sum(qm * kn_ref[0], axis=1, keepdims=True) * scale
        l_ref[...] = jnp.ones(l_ref.shape, jnp.float32)
        acc_ref[...] = jnp.broadcast_to(vn_ref[0], (N_HEADS, d_att))
        eye = (lax.broadcasted_iota(jnp.int32, (N_HEADS, N_HEADS), 0)
               == lax.broadcasted_iota(jnp.int32, (N_HEADS, N_HEADS), 1))
        d_ref[...] = jnp.sum(jnp.where(eye, jnp.broadcast_to(lfn_ref[0], (N_HEADS, N_HEADS)), 0.0),
                             axis=1, keepdims=True)

    decay = d_ref[...]
    qm = qm_ref[...]
    ss = []
    for r in range(pp):
        lf = clf_refs[r][0, 0]
        suf = _lane_suffix_sum(lf)
        ss.append(_bdot_nt(qm, ck_refs[r][0, 0]) * scale + (decay + (suf - lf)))
        decay = decay + suf[:, 0:1]
    d_ref[...] = decay
    m_old = m_ref[...]
    m_new = m_old
    for s in ss:
        m_new = jnp.maximum(m_new, jnp.max(s, axis=1, keepdims=True))
    a = jnp.exp(m_old - m_new)
    l_new = a * l_ref[...]
    acc = a * acc_ref[...]
    for r in range(pp):
        p = jnp.exp(ss[r] - m_new)
        l_new = l_new + jnp.sum(p, axis=1, keepdims=True)
        acc = acc + _bdot(p, cv_refs[r][0, 0])
    l_ref[...] = l_new
    acc_ref[...] = acc
    m_ref[...] = m_new

    @pl.when(j == pl.num_programs(1) - 1)
    def _():
        o = acc_ref[...] / l_ref[...]
        o_ref[0] = jnp.sum(jnp.where(own, o, 0.0), axis=0, keepdims=True)


def _fox_sample(q, k_new, v_new, lf_new, cache_k, cache_v, cache_lf_t, page_table, layer):
    n, d_att = q.shape
    n_pages = page_table.shape[1]
    pp = math.gcd(n_pages, DECODE_PAGES_PER_STEP)
    per_seq = lambda shp: pl.BlockSpec((1,) + shp, lambda b, j, pt: (b, 0, 0))

    def paged(shp, r):
        return pl.BlockSpec((1, 1) + shp, lambda b, j, pt: (pt[b, n_pages - 1 - (j * pp + r)], layer, 0, 0))

    grid_spec = pltpu.PrefetchScalarGridSpec(
        num_scalar_prefetch=1,
        grid=(n, n_pages // pp),
        in_specs=([per_seq((1, d_att)), per_seq((1, d_att)), per_seq((1, d_att)), per_seq((1, N_HEADS))]
                  + [paged((PAGE, d_att), r) for r in range(pp)] + [paged((PAGE, d_att), r) for r in range(pp)]
                  + [paged((N_HEADS, PAGE), r) for r in range(pp)]),
        out_specs=per_seq((1, d_att)),
        scratch_shapes=[pltpu.VMEM((N_HEADS, d_att), jnp.float32), pltpu.VMEM((N_HEADS, 1), jnp.float32),
                        pltpu.VMEM((N_HEADS, 1), jnp.float32), pltpu.VMEM((N_HEADS, d_att), jnp.float32),
                        pltpu.VMEM((N_HEADS, 1), jnp.float32)],
    )
    o = pl.pallas_call(
        functools.partial(_fox_sample_kernel, scale=HEAD_DIM ** -0.5, pp=pp),
        grid_spec=grid_spec,
        out_shape=jax.ShapeDtypeStruct((n, 1, d_att), jnp.float32),
        compiler_params=_params("arbitrary", "arbitrary"),
        name="fox_sample",
    )(page_table, q.reshape(n, 1, d_att), k_new.reshape(n, 1, d_att), v_new.reshape(n, 1, d_att),
      lf_new.reshape(n, 1, N_HEADS), *([cache_k] * pp), *([cache_v] * pp), *([cache_lf_t] * pp))
    return o.reshape(n, d_att)


def _sg_prompt_kernel(u_ref, v_ref, g_ref, b_ref, ws_ref, bs_ref, o_ref, *, tm):
    v = _layer_norm(_gelu(v_ref[...]), g_ref[...], b_ref[...])
    tri = (lax.broadcasted_iota(jnp.int32, (CHUNK, CHUNK), 1)
           <= lax.broadcasted_iota(jnp.int32, (CHUNK, CHUNK), 0))
    gd = v.shape[1] // SG_GROUPS
    for g in range(SG_GROUPS):
        w = jnp.where(tri, ws_ref[g], 0.0)
        bias = bs_ref[:, g:g + 1]
        cols = slice(g * gd, (g + 1) * gd)
        for c in range(tm // CHUNK):
            rows = slice(c * CHUNK, (c + 1) * CHUNK)
            y = _bdot(w, v[rows, cols]) + bias
            o_ref[rows, cols] = _gelu(u_ref[rows, cols]) * y


def _sg_prompt(hw, ln_g, ln_b, w_s, b_s_t):
    n, d2 = hw.shape
    d = d2 // 2
    tm = 512
    fixed = lambda shp: pl.BlockSpec(shp, lambda i: (0,) * len(shp))
    return pl.pallas_call(
        functools.partial(_sg_prompt_kernel, tm=tm),
        grid=(n // tm,),
        in_specs=[pl.BlockSpec((tm, d), lambda i: (i, 0)), pl.BlockSpec((tm, d), lambda i: (i, 1)),
                  fixed((1, d)), fixed((1, d)), fixed(w_s.shape), fixed(b_s_t.shape)],
        out_specs=pl.BlockSpec((tm, d), lambda i: (i, 0)),
        out_shape=jax.ShapeDtypeStruct((n, d), jnp.float32),
        compiler_params=_params("arbitrary"),
        name="sg_prompt",
    )(hw, hw, ln_g.reshape(1, d), ln_b.reshape(1, d), w_s, b_s_t)


def _sg_sample_kernel(u_ref, v_ref, g_ref, b_ref, wd_ref, b0_ref, o_ref, vo_ref):
    v = _layer_norm(_gelu(v_ref[...]), g_ref[...], b_ref[...])
    vo_ref[...] = v
    o_ref[...] = _gelu(u_ref[...]) * (wd_ref[...] * v + b0_ref[...])


def _sg_sample(hw, ln_g, ln_b, w_diag0, b0):
    n, d2 = hw.shape
    d = d2 // 2
    fixed = lambda shp: pl.BlockSpec(shp, lambda i: (0,) * len(shp))
    return pl.pallas_call(
        _sg_sample_kernel,
        grid=(1,),
        in_specs=[pl.BlockSpec((n, d), lambda i: (0, 0)), pl.BlockSpec((n, d), lambda i: (0, 1)),
                  fixed((1, d)), fixed((1, d)), fixed((1, d)), fixed((1, d))],
        out_specs=[fixed((n, d)), fixed((n, d))],
        out_shape=[jax.ShapeDtypeStruct((n, d), jnp.float32), jax.ShapeDtypeStruct((n, d), jnp.float32)],
        compiler_params=_params("arbitrary"),
        name="sg_sample",
    )(hw, hw, ln_g.reshape(1, d), ln_b.reshape(1, d), w_diag0.reshape(1, d), b0.reshape(1, d))


def _top16(s, pos, ids):
    far = jnp.iinfo(jnp.int32).max
    vals, outs = [], []
    for _ in range(PEER_TOPK):
        m = jnp.max(s, axis=0, keepdims=True)
        am = jnp.min(jnp.where(s == m, pos, far), axis=0, keepdims=True)
        hit = pos == am
        vals.append(m)
        outs.append(am if ids is None else jnp.max(jnp.where(hit, ids, -1), axis=0, keepdims=True))
        s = jnp.where(hit, NEG_INF, s)
    return jnp.concatenate(vals, axis=0), jnp.concatenate(outs, axis=0)


def _peer_topk_kernel(q_ref, keys_ref, e_ref, g_ref):
    tb = q_ref.shape[0]
    key_pos = lax.broadcasted_iota(jnp.int32, (PEER_NKEYS, tb), 0)
    half = []
    for p in range(2):
        s = _bdot_nt(keys_ref[0, p], q_ref[:, p * PEER_NKEYS:(p + 1) * PEER_NKEYS])
        half.append(_top16(s, key_pos, None))
    (s0, i0), (s1, i1) = half
    row8 = lax.broadcasted_iota(jnp.int32, (SUBLANES, tb), 0)
    row16 = lax.broadcasted_iota(jnp.int32, (PEER_TOPK, tb), 0)
    cs, ci, cp = [s0[0:1, :] + s1], [i0[0:1, :] * PEER_NKEYS + i1], [row16]
    for a in range(1, SUBLANES):
        keep = row8 < PEER_TOPK // (a + 1)
        cs.append(jnp.where(keep, s0[a:a + 1, :] + s1[0:SUBLANES, :], NEG_INF))
        ci.append(i0[a:a + 1, :] * PEER_NKEYS + i1[0:SUBLANES, :])
        cp.append(jnp.where(keep, a * PEER_TOPK + row8, PEER_TOPK * PEER_TOPK + a * PEER_TOPK + row8))
    cs.append(s0[SUBLANES:, :] + s1[0:1, :])
    ci.append(i0[SUBLANES:, :] * PEER_NKEYS + i1[0:1, :])
    cp.append((row8 + SUBLANES) * PEER_TOPK)
    best_s, experts = _top16(jnp.concatenate(cs, axis=0), jnp.concatenate(cp, axis=0), jnp.concatenate(ci, axis=0))
    e = jnp.exp(best_s - best_s[0:1, :])
    e_ref[0] = experts
    g_ref[0] = e / jnp.sum(e, axis=0, keepdims=True)


def _peer_topk(q, keys):
    n = q.shape[0]
    tb = _row_block(n, 256)
    dk = 2 * PEER_NKEYS
    out = pl.BlockSpec((1, PEER_TOPK, tb), lambda i, h: (h, 0, i))
    e, g = pl.pallas_call(
        _peer_topk_kernel,
        grid=(n // tb, PEER_HEADS),
        in_specs=[pl.BlockSpec((tb, dk), lambda i, h: (i, h)),
                  pl.BlockSpec((1, 2, PEER_NKEYS, dk // 2), lambda i, h: (h, 0, 0, 0))],
        out_specs=[out, out],
        out_shape=[jax.ShapeDtypeStruct((PEER_HEADS, PEER_TOPK, n), jnp.int32),
                   jax.ShapeDtypeStruct((PEER_HEADS, PEER_TOPK, n), jnp.float32)],
        compiler_params=_params("arbitrary", "arbitrary"),
        name="peer_topk",
    )(q, keys)
    return e.reshape(PEER_SLOTS, n), g.reshape(PEER_SLOTS, n)


def _tree_sum(xs):
    while len(xs) > 1:
        xs = [xs[k] + xs[k + 1] for k in range(0, len(xs) - 1, 2)] + ([xs[-1]] if len(xs) % 2 else [])
    return xs[0]


def _tile_mean(y):
    s = jnp.sum(jnp.sum(y, axis=-1, keepdims=True), axis=-2, keepdims=True)
    return s * (1.0 / (y.shape[-1] * y.shape[-2]))


def _fold_sublanes(a, b, r):
    low = (lax.broadcasted_iota(jnp.int32, a.shape, 0) % (2 * r)) < r
    return jnp.where(low, a, pltpu.roll(b, r, 0)) + jnp.where(low, pltpu.roll(a, SUBLANES - r, 0), b)


_FOLD_ORDER = (0, 4, 2, 6, 1, 5, 3, 7)


def _sublane_sums(tiles):
    v = [tiles[_FOLD_ORDER.index(k)] for k in range(SUBLANES)]
    r = SUBLANES // 2
    while len(v) > 1:
        v = [_fold_sublanes(v[2 * k], v[2 * k + 1], r) for k in range(len(v) // 2)]
        r //= 2
    return v[0]


def _peer_gather_kernel(idx_ref, x_ref, g_ref, lg_ref, lb_ref, uv_ref, o_ref, buf0_ref, buf1_ref, sem_ref,
                        *, tb, d, alpha):
    i = pl.program_id(0)
    n_groups = tb // GATHER_GROUP
    group_picks = GATHER_GROUP * PEER_SLOTS
    n_lane_tiles = d // LANES
    bufs = (buf0_ref, buf1_ref)

    def start(k, slot, t, row, thread=0):
        pltpu.make_async_copy(uv_ref.at[idx_ref[k]], bufs[slot].at[row], sem_ref.at[slot, t]).start(priority=thread)

    def wait(slot, t):
        rows = bufs[slot].at[pl.ds(t * PEER_SLOTS, PEER_SLOTS)]
        pltpu.make_async_copy(rows, rows, sem_ref.at[slot, t]).wait()

    def mix_token(tok, t, slot, pick0, slot_next):
        buf = bufs[slot]
        x = x_ref[tok]
        lane = lax.broadcasted_iota(jnp.int32, (SUBLANES, tb), 1)
        ys = []
        tok_row0 = t * PEER_SLOTS
        tok_pick0 = pick0 + tok_row0
        for r in range(PEER_SLOTS // SUBLANES):
            row0 = tok_row0 + r * SUBLANES
            for jj in range(SUBLANES):
                start(tok_pick0 + (r * SUBLANES + jj), slot_next, t, row0 + jj, thread=jj % DMA_THREADS)
            words = [buf[row0 + jj] for jj in range(SUBLANES)]
            us = [lax.bitcast_convert_type(wd << 16, jnp.float32) for wd in words]
            vs = [lax.bitcast_convert_type(wd & jnp.uint32(0xFFFF0000), jnp.float32) for wd in words]
            h = jnp.sum(_sublane_sums([u * x for u in us]), axis=1, keepdims=True)
            gate = jnp.sum(jnp.where(lane == tok, g_ref[0, r * SUBLANES:(r + 1) * SUBLANES, :], 0.0),
                           axis=1, keepdims=True)
            w = jnp.broadcast_to(gate * _gelu(h), (SUBLANES, LANES))
            ys.append(_tree_sum([jnp.broadcast_to(w[jj:jj + 1, :], (SUBLANES, LANES)) * vs[jj]
                                 for jj in range(SUBLANES)]))
        o_ref[tok] = alpha * x + _tree_sum(ys)

    def mix_group(grp, slot, slot_next):
        def body(tt, c):
            for dt in range(TOKENS_PER_ITER):
                wait(slot, tt * TOKENS_PER_ITER + dt)
            for dt in range(TOKENS_PER_ITER):
                t = tt * TOKENS_PER_ITER + dt
                mix_token(grp * GATHER_GROUP + t, t, slot, (grp + 1) * group_picks, slot_next)
            return c
        lax.fori_loop(0, GATHER_GROUP // TOKENS_PER_ITER, body, 0)

    @pl.when(i == 0)
    def _():
        def body(k, c):
            start(k, 0, k // PEER_SLOTS, k)
            return c
        lax.fori_loop(0, group_picks, body, 0, unroll=8)

    def pair(gg, c):
        mix_group(2 * gg, 0, 1)
        mix_group(2 * gg + 1, 1, 0)
        return c
    lax.fori_loop(0, n_groups // 2, pair, 0)

    @pl.when(i == pl.num_programs(0) - 1)
    def _():
        for t in range(GATHER_GROUP):
            wait(0, t)

    y = o_ref[...]
    yc = y - _tile_mean(y)
    o_ref[...] = yc * lax.rsqrt(_tile_mean(yc * yc) + LN_EPS) * lg_ref[...] + lb_ref[...]


def _pack_bf16_pair(lo, hi):
    bits = lambda a: lax.bitcast_convert_type(a.astype(jnp.bfloat16), jnp.uint16).astype(jnp.uint32)
    return bits(lo) | (bits(hi) << 16)


def _peer_gather_ln(x, experts, gates, uv, ln_g, ln_b, alpha):
    n, d = x.shape
    tb = _row_block(n, GATHER_BLOCK)
    nblk = n // tb
    assert (tb // GATHER_GROUP) % 2 == 0 and d == SUBLANES * LANES
    group_picks = GATHER_GROUP * PEER_SLOTS
    picks = experts.T.reshape(nblk, tb * PEER_SLOTS)
    picks = jnp.concatenate([picks, jnp.roll(picks[:, :group_picks], -1, axis=0)], axis=1).reshape(-1)
    buf = pltpu.VMEM((group_picks, SUBLANES, LANES), jnp.uint32)
    tile = (SUBLANES, LANES)
    out = pl.pallas_call(
        functools.partial(_peer_gather_kernel, tb=tb, d=d, alpha=alpha),
        grid=(nblk,),
        in_specs=[pl.BlockSpec((tb * PEER_SLOTS + group_picks,), lambda i: (i,), memory_space=pltpu.SMEM),
                  pl.BlockSpec((tb,) + tile, lambda i: (i, 0, 0)),
                  pl.BlockSpec((1, PEER_SLOTS, tb), lambda i: (i, 0, 0)),
                  pl.BlockSpec(tile, lambda i: (0, 0)), pl.BlockSpec(tile, lambda i: (0, 0)),
                  pl.BlockSpec(memory_space=pl.ANY)],
        out_specs=pl.BlockSpec((tb,) + tile, lambda i: (i, 0, 0)),
        out_shape=jax.ShapeDtypeStruct((n,) + tile, jnp.float32),
        scratch_shapes=[buf, buf, pltpu.SemaphoreType.DMA((2, GATHER_GROUP))],
        compiler_params=_params("arbitrary", disable_bounds_checks=True),
        name="peer_gather",
    )(picks, x.reshape((n,) + tile), jnp.swapaxes(gates.reshape(PEER_SLOTS, nblk, tb), 0, 1),
      ln_g.reshape(tile), ln_b.reshape(tile), uv)
    return out.reshape(n, d)


def kernel(x_prompt, x_sample, cache_k, cache_v, cache_logf, state_conv, page_table, w_in_ab, b_f, conv_w, w_o_ab,
           w_in_c, sg_ln_g, sg_ln_b, w_s, b_s, w_o_c, ln1_g, ln1_b, ln2_g, ln2_b, peer_wq, peer_keys, peer_u, peer_v):
    n_seq, t, d = x_prompt.shape
    n_dec = x_sample.shape[0]
    depth = ln1_g.shape[0]
    d_conv = conv_w.shape[2]
    d_att = N_HEADS * HEAD_DIM
    d_in = w_in_ab.shape[2]
    d_in_pad = -(-d_in // (5 * LANES)) * (5 * LANES)
    alpha = (2.0 * depth) ** 0.25
    bf16 = jnp.bfloat16

    xp = x_prompt.reshape(n_seq * t, d)
    xs = x_sample.reshape(n_dec, d)
    n_phys = cache_k.shape[0]
    ck = cache_k.reshape(n_phys, -1, PAGE, d_att)
    cv = cache_v.reshape(n_phys, -1, PAGE, d_att)
    clf_t = jnp.swapaxes(cache_logf, 2, 3)

    kp_l, vp_l, fp_l, cp_l, ks_l, vs_l, fs_l, cs_l, sv_l = [], [], [], [], [], [], [], [], []
    for l in range(depth):
        i = l // 2
        if l % 2 == 0:
            w_in = jnp.pad(w_in_ab[i], ((0, 0), (0, d_in_pad - d_in))).astype(bf16)
            w_o = w_o_ab[i].astype(bf16)
            xw = _matmul(xp, w_in, d_in_pad // 5)
            a_out, lf, conv_new = _conv_prompt(xw, b_f[i], conv_w[i], n_seq, d_conv)
            cum_t = _cumsum_time(jnp.swapaxes(lf.reshape(n_seq, t, N_HEADS), 1, 2))
            cum = jnp.swapaxes(cum_t, 1, 2).reshape(n_seq * t, N_HEADS)
            o = _fox_prompt(xw, cum, cum_t, n_seq, d_conv)
            kp_l.append(xw[:, 3 * d_conv + d_att:3 * d_conv + 2 * d_att].reshape(n_seq, t, N_HEADS, HEAD_DIM))
            vp_l.append(xw[:, 3 * d_conv + 2 * d_att:3 * d_conv + 3 * d_att].reshape(n_seq, t, N_HEADS, HEAD_DIM))
            fp_l.append(lf.reshape(n_seq, t, N_HEADS))
            cp_l.append(conv_new)
            xp = _matmul_res_ln([a_out, o], [w_o[:d_conv], w_o[d_conv:]], xp, ln1_g[l], ln1_b[l], alpha)
            xw = _matmul(xs, w_in, d_in_pad // 5)
            a_out, lf, z = _conv_sample(xw, state_conv[:, i, 0], state_conv[:, i, 1], b_f[i], conv_w[i], d_conv)
            q = xw[:, 3 * d_conv:3 * d_conv + d_att]
            k = xw[:, 3 * d_conv + d_att:3 * d_conv + 2 * d_att]
            v = xw[:, 3 * d_conv + 2 * d_att:3 * d_conv + 3 * d_att]
            o = _fox_sample(q, k, v, lf, ck, cv, clf_t, page_table, i)
            ks_l.append(k.reshape(n_dec, 1, N_HEADS, HEAD_DIM))
            vs_l.append(v.reshape(n_dec, 1, N_HEADS, HEAD_DIM))
            fs_l.append(lf.reshape(n_dec, 1, N_HEADS))
            cs_l.append(jnp.stack([state_conv[:, i, 1], z], axis=1))
            xs = _matmul_res_ln([a_out, o], [w_o[:d_conv], w_o[d_conv:]], xs, ln1_g[l], ln1_b[l], alpha)
        else:
            w_in = w_in_c[i].astype(bf16)
            w_o = w_o_c[i].astype(bf16)
            d_sg = w_o.shape[0]
            hw = _matmul(xp, w_in, d_sg)
            sg = _sg_prompt(hw, sg_ln_g[i], sg_ln_b[i], w_s[i], b_s[i].T)
            xp = _matmul_res_ln([sg], [w_o], xp, ln1_g[l], ln1_b[l], alpha)
            hw = _matmul(xs, w_in, d_sg)
            gd = d_sg // SG_GROUPS
            sg, sv = _sg_sample(hw, sg_ln_g[i], sg_ln_b[i], jnp.repeat(w_s[i, :, 0, 0], gd), jnp.repeat(b_s[i, :, 0], gd))
            sv_l.append(sv.reshape(n_dec, 1, d_sg))
            xs = _matmul_res_ln([sg], [w_o], xs, ln1_g[l], ln1_b[l], alpha)
        wq = peer_wq[l].astype(bf16)
        uv = _pack_bf16_pair(peer_u[l], peer_v[l]).reshape(-1, SUBLANES, LANES)
        for which in range(2):
            xx = xp if which == 0 else xs
            qq = _matmul(xx, wq, wq.shape[1] // 2)
            experts, gates = _peer_topk(qq, peer_keys[l])
            xx = _peer_gather_ln(xx, experts, gates, uv, ln2_g[l], ln2_b[l], alpha)
            if which == 0:
                xp = xx
            else:
                xs = xx

    return (xp.reshape(n_seq, t, d), xs.reshape(n_dec, 1, d),
            jnp.stack(kp_l, axis=1), jnp.stack(vp_l, axis=1), jnp.stack(fp_l, axis=1),
            jnp.stack(ks_l, axis=1), jnp.stack(vs_l, axis=1), jnp.stack(fs_l, axis=1),
            jnp.stack(cp_l, axis=1), jnp.stack(cs_l, axis=1), jnp.stack(sv_l, axis=1))
```

```python
import functools
import math

import jax
import jax.numpy as jnp
from jax import lax
from jax.experimental import pallas as pl
from jax.experimental.pallas import tpu as pltpu

LANES = 128
SUBLANES = 8
VMEM_LIMIT = 48 * 1024 * 1024

LN_EPS = 1e-5
HEAD_DIM = 64
N_HEADS = 8
PAGE = 128
CHUNK = 128
SG_GROUPS = 8
PEER_HEADS = 8
PEER_NKEYS = 128
PEER_TOPK = 16
PEER_SLOTS = PEER_HEADS * PEER_TOPK
GATHER_GROUP = 8
DMA_THREADS = 2
GATHER_BLOCK = 64
TOKENS_PER_ITER = 4
DECODE_PAGES_PER_STEP = 8
NEG_INF = float("-inf")


def _params(*sem, **kw):
    return pltpu.CompilerParams(dimension_semantics=sem, vmem_limit_bytes=VMEM_LIMIT, **kw)


def _gelu(x):
    return 0.5 * x * (1.0 + jnp.tanh(math.sqrt(2.0 / math.pi) * (x + 0.044715 * (x * x * x))))


def _layer_norm(x, g, b):
    mu = jnp.mean(x, axis=-1, keepdims=True)
    xc = x - mu
    var = jnp.mean(xc * xc, axis=-1, keepdims=True)
    return xc * lax.rsqrt(var + LN_EPS) * g + b


def _log_sigmoid(x):
    return -(jnp.maximum(-x, 0.0) + jnp.log1p(jnp.exp(-jnp.abs(x))))


def _bdot(a, b):
    return jnp.dot(a.astype(jnp.bfloat16), b.astype(jnp.bfloat16), preferred_element_type=jnp.float32)


def _bdot_nt(a, b):
    return lax.dot_general(a.astype(jnp.bfloat16), b.astype(jnp.bfloat16), (((1,), (1,)), ((), ())),
                           preferred_element_type=jnp.float32)


def _row_block(n, pref):
    return pref if n % pref == 0 else n


def _mm_kernel(x_ref, w_ref, o_ref):
    o_ref[...] = _bdot(x_ref[...], w_ref[...])


def _matmul(x, w, tn):
    n, k = x.shape
    m = w.shape[1]
    tm = _row_block(n, 512)
    return pl.pallas_call(
        _mm_kernel,
        grid=(m // tn, n // tm),
        in_specs=[pl.BlockSpec((tm, k), lambda j, i: (i, 0)),
                  pl.BlockSpec((k, tn), lambda j, i: (0, j))],
        out_specs=pl.BlockSpec((tm, tn), lambda j, i: (i, j)),
        out_shape=jax.ShapeDtypeStruct((n, m), jnp.float32),
        compiler_params=_params("arbitrary", "arbitrary"),
        name="matmul",
    )(x, w)


def _mm_res_ln_kernel(*refs, n_in, alpha):
    xs, ws = refs[:n_in], refs[n_in:2 * n_in]
    res_ref, g_ref, b_ref, o_ref = refs[2 * n_in:]
    y = alpha * res_ref[...]
    for x_ref, w_ref in zip(xs, ws):
        y = y + _bdot(x_ref[...], w_ref[...])
    o_ref[...] = _layer_norm(y, g_ref[...], b_ref[...])


def _matmul_res_ln(xs, ws, res, g, b, alpha):
    n, d = res.shape
    tm = _row_block(n, 512)
    row = lambda i: (i, 0)
    fixed = lambda i: (0, 0)
    in_specs = ([pl.BlockSpec((tm, x.shape[1]), row) for x in xs]
                + [pl.BlockSpec(w.shape, fixed) for w in ws]
                + [pl.BlockSpec((tm, d), row), pl.BlockSpec((1, d), fixed), pl.BlockSpec((1, d), fixed)])
    return pl.pallas_call(
        functools.partial(_mm_res_ln_kernel, n_in=len(xs), alpha=alpha),
        grid=(n // tm,),
        in_specs=in_specs,
        out_specs=pl.BlockSpec((tm, d), row),
        out_shape=jax.ShapeDtypeStruct((n, d), jnp.float32),
        compiler_params=_params("arbitrary"),
        name="matmul_res_ln",
    )(*xs, *ws, res, g.reshape(1, d), b.reshape(1, d))


def _conv_prompt_kernel(gb_ref, gc_ref, hc_ref, gcp_ref, hcp_ref, fl_ref, bf_ref, cw_ref,
                        a_ref, lf_ref, cn_ref, zs_ref, *, tm, blocks_per_seq):
    i = pl.program_id(0)
    z = gc_ref[...] * hc_ref[...]
    zprev = gcp_ref[...] * hcp_ref[...]
    zprev = jnp.where(i % blocks_per_seq == 0, jnp.zeros_like(zprev), zprev)
    zs_ref[0:SUBLANES, :] = zprev
    zs_ref[SUBLANES:, :] = z
    z1 = zs_ref[pl.ds(SUBLANES - 1, tm), :]
    z2 = zs_ref[pl.ds(SUBLANES - 2, tm), :]
    cw = cw_ref[...]
    y = cw[0:1, :] * z2 + cw[1:2, :] * z1 + cw[2:3, :] * z
    a_ref[...] = gb_ref[...] * y
    lf_ref[...] = _log_sigmoid(fl_ref[:, 0:N_HEADS] + bf_ref[...])
    cn_ref[0] = zs_ref[pl.ds(tm + SUBLANES - 2, 2), :]


def _conv_prompt(xw, b_f, conv_w, n_seq, d_conv):
    n = xw.shape[0]
    t = n // n_seq
    tm = 512
    bps = t // tm
    fl_blk = (6 * d_conv) // LANES
    col = lambda c: (lambda i: (i, c))
    prev = lambda c: (lambda i: (jnp.maximum(i * (tm // SUBLANES) - 1, 0), c))
    return pl.pallas_call(
        functools.partial(_conv_prompt_kernel, tm=tm, blocks_per_seq=bps),
        grid=(n // tm,),
        in_specs=[pl.BlockSpec((tm, d_conv), col(0)), pl.BlockSpec((tm, d_conv), col(1)),
                  pl.BlockSpec((tm, d_conv), col(2)),
                  pl.BlockSpec((SUBLANES, d_conv), prev(1)), pl.BlockSpec((SUBLANES, d_conv), prev(2)),
                  pl.BlockSpec((tm, LANES), col(fl_blk)),
                  pl.BlockSpec((1, N_HEADS), lambda i: (0, 0)),
                  pl.BlockSpec(conv_w.shape, lambda i: (0, 0))],
        out_specs=[pl.BlockSpec((tm, d_conv), lambda i: (i, 0)),
                   pl.BlockSpec((tm, N_HEADS), lambda i: (i, 0)),
                   pl.BlockSpec((1, 2, d_conv), lambda i: (i // bps, 0, 0))],
        out_shape=[jax.ShapeDtypeStruct((n, d_conv), jnp.float32),
                   jax.ShapeDtypeStruct((n, N_HEADS), jnp.float32),
                   jax.ShapeDtypeStruct((n_seq, 2, d_conv), jnp.float32)],
        scratch_shapes=[pltpu.VMEM((tm + SUBLANES, d_conv), jnp.float32)],
        compiler_params=_params("arbitrary"),
        name="conv_prompt",
    )(xw, xw, xw, xw, xw, xw, b_f.reshape(1, N_HEADS), conv_w)


def _conv_sample_kernel(gb_ref, gc_ref, hc_ref, fl_ref, p0_ref, p1_ref, bf_ref, cw_ref, a_ref, lf_ref, z_ref):
    z = gc_ref[...] * hc_ref[...]
    cw = cw_ref[...]
    y = cw[0:1, :] * p0_ref[...] + cw[1:2, :] * p1_ref[...] + cw[2:3, :] * z
    a_ref[...] = gb_ref[...] * y
    lf_ref[...] = _log_sigmoid(fl_ref[:, 0:N_HEADS] + bf_ref[...])
    z_ref[...] = z


def _conv_sample(xw, prev0, prev1, b_f, conv_w, d_conv):
    n = xw.shape[0]
    fl_blk = (6 * d_conv) // LANES
    full = lambda shp: pl.BlockSpec(shp, lambda i: (0,) * len(shp))
    return pl.pallas_call(
        _conv_sample_kernel,
        grid=(1,),
        in_specs=[pl.BlockSpec((n, d_conv), lambda i: (0, 0)), pl.BlockSpec((n, d_conv), lambda i: (0, 1)),
                  pl.BlockSpec((n, d_conv), lambda i: (0, 2)), pl.BlockSpec((n, LANES), lambda i: (0, fl_blk)),
                  full((n, d_conv)), full((n, d_conv)), full((1, N_HEADS)), full(conv_w.shape)],
        out_specs=[full((n, d_conv)), full((n, N_HEADS)), full((n, d_conv))],
        out_shape=[jax.ShapeDtypeStruct((n, d_conv), jnp.float32),
                   jax.ShapeDtypeStruct((n, N_HEADS), jnp.float32),
                   jax.ShapeDtypeStruct((n, d_conv), jnp.float32)],
        compiler_params=_params("arbitrary"),
        name="conv_sample",
    )(xw, xw, xw, xw, prev0, prev1, b_f.reshape(1, N_HEADS), conv_w)


def _lane_prefix_sum(x):
    lane = lax.broadcasted_iota(jnp.int32, x.shape, 1)
    d = 1
    while d < LANES:
        x = x + jnp.where(lane >= d, pltpu.roll(x, d, 1), 0.0)
        d *= 2
    return x


def _lane_suffix_sum(x):
    lane = lax.broadcasted_iota(jnp.int32, x.shape, 1)
    d = 1
    while d < LANES:
        x = x + jnp.where(lane < LANES - d, pltpu.roll(x, LANES - d, 1), 0.0)
        d *= 2
    return x


def _cumsum_kernel(x_ref, o_ref, *, t):
    carry = jnp.zeros((N_HEADS, 1), jnp.float32)
    for c in range(t // LANES):
        sl = slice(c * LANES, (c + 1) * LANES)
        s = _lane_prefix_sum(x_ref[0, :, sl]) + carry
        o_ref[0, :, sl] = s
        carry = s[:, LANES - 1:LANES]


def _cumsum_time(lf_t):
    n_seq, h, t = lf_t.shape
    return pl.pallas_call(
        functools.partial(_cumsum_kernel, t=t),
        grid=(n_seq,),
        in_specs=[pl.BlockSpec((1, h, t), lambda b: (b, 0, 0))],
        out_specs=pl.BlockSpec((1, h, t), lambda b: (b, 0, 0)),
        out_shape=jax.ShapeDtypeStruct(lf_t.shape, jnp.float32),
        compiler_params=_params("arbitrary"),
        name="cumsum_time",
    )(lf_t)


def _fox_prompt_kernel(q_ref, k_ref, v_ref, cq_ref, ck_ref, o_ref, m_ref, l_ref, acc_ref, *, tq, scale):
    qb, kb = pl.program_id(1), pl.program_id(2)

    @pl.when(kb == 0)
    def _():
        m_ref[...] = jnp.full(m_ref.shape, NEG_INF, jnp.float32)
        l_ref[...] = jnp.zeros(l_ref.shape, jnp.float32)
        acc_ref[...] = jnp.zeros(acc_ref.shape, jnp.float32)

    @pl.when(kb <= qb)
    def _():
        row = qb * tq + lax.broadcasted_iota(jnp.int32, (tq, tq), 0)
        colp = kb * tq + lax.broadcasted_iota(jnp.int32, (tq, tq), 1)
        causal = colp <= row
        lane = lax.broadcasted_iota(jnp.int32, (1, LANES), 1)
        for h in range(N_HEADS):
            pair = slice((h // 2) * LANES, (h // 2 + 1) * LANES)
            own = (lane // HEAD_DIM) == (h % 2)
            qh = jnp.where(own, q_ref[:, pair], 0.0)
            s = _bdot_nt(qh, k_ref[:, pair]) * scale
            s = s + cq_ref[:, h:h + 1] - ck_ref[0, h:h + 1, :]
            s = jnp.where(causal, s, NEG_INF)
            m_old = m_ref[h]
            m_new = jnp.maximum(m_old, jnp.max(s, axis=1, keepdims=True))
            p = jnp.exp(s - m_new)
            a = jnp.exp(m_old - m_new)
            l_ref[h] = a * l_ref[h] + jnp.sum(p, axis=1, keepdims=True)
            acc_ref[h] = a * acc_ref[h] + _bdot(p, v_ref[:, pair])
            m_ref[h] = m_new

    @pl.when(kb == qb)
    def _():
        lane = lax.broadcasted_iota(jnp.int32, (1, LANES), 1)
        for hp in range(N_HEADS // 2):
            o0 = acc_ref[2 * hp] / l_ref[2 * hp]
            o1 = acc_ref[2 * hp + 1] / l_ref[2 * hp + 1]
            o_ref[:, hp * LANES:(hp + 1) * LANES] = jnp.where(lane < HEAD_DIM, o0, o1)


def _fox_prompt(xw, cum, cum_t, n_seq, d_conv):
    n = xw.shape[0]
    t = n // n_seq
    d_att = N_HEADS * HEAD_DIM
    tq = 512
    nb = t // tq
    qc, kc, vc = (3 * d_conv) // d_att, (3 * d_conv) // d_att + 1, (3 * d_conv) // d_att + 2
    return pl.pallas_call(
        functools.partial(_fox_prompt_kernel, tq=tq, scale=HEAD_DIM ** -0.5),
        grid=(n_seq, nb, nb),
        in_specs=[pl.BlockSpec((tq, d_att), lambda b, i, j: (b * nb + i, qc)),
                  pl.BlockSpec((tq, d_att), lambda b, i, j: (b * nb + jnp.minimum(i, j), kc)),
                  pl.BlockSpec((tq, d_att), lambda b, i, j: (b * nb + jnp.minimum(i, j), vc)),
                  pl.BlockSpec((tq, N_HEADS), lambda b, i, j: (b * nb + i, 0)),
                  pl.BlockSpec((1, N_HEADS, tq), lambda b, i, j: (b, 0, jnp.minimum(i, j)))],
        out_specs=pl.BlockSpec((tq, d_att), lambda b, i, j: (b * nb + i, 0)),
        out_shape=jax.ShapeDtypeStruct((n, d_att), jnp.float32),
        scratch_shapes=[pltpu.VMEM((N_HEADS, tq, 1), jnp.float32), pltpu.VMEM((N_HEADS, tq, 1), jnp.float32),
                        pltpu.VMEM((N_HEADS, tq, LANES), jnp.float32)],
        compiler_params=_params("arbitrary", "arbitrary", "arbitrary"),
        name="fox_prompt",
    )(xw, xw, xw, cum, cum_t)


def _fox_sample_kernel(pt_ref, q_ref, kn_ref, vn_ref, lfn_ref, *refs, scale, pp):
    ck_refs, cv_refs, clf_refs = refs[:pp], refs[pp:2 * pp], refs[2 * pp:3 * pp]
    o_ref, qm_ref, m_ref, l_ref, acc_ref, d_ref = refs[3 * pp:]
    j = pl.program_id(1)
    d_att = N_HEADS * HEAD_DIM
    head_of_lane = lax.broadcasted_iota(jnp.int32, (N_HEADS, d_att), 1) // HEAD_DIM
    head_of_row = lax.broadcasted_iota(jnp.int32, (N_HEADS, d_att), 0)
    own = head_of_lane == head_of_row

    @pl.when(j == 0)
    def _():
        qm = jnp.where(own, jnp.broadcast_to(q_ref[0], (N_HEADS, d_att)), 0.0)
        qm_ref[...] = qm
        m_ref[...] = jnp.sum(qm * kn_ref[0], axis=1, keepdims=True) * scale
        l_ref[...] = jnp.ones(l_ref.shape, jnp.float32)
        acc_ref[...] = jnp.broadcast_to(vn_ref[0], (N_HEADS, d_att))
        eye = (lax.broadcasted_iota(jnp.int32, (N_HEADS, N_HEADS), 0)
               == lax.broadcasted_iota(jnp.int32, (N_HEADS, N_HEADS), 1))
        d_ref[...] = jnp.sum(jnp.where(eye, jnp.broadcast_to(lfn_ref[0], (N_HEADS, N_HEADS)), 0.0),
                             axis=1, keepdims=True)

    decay = d_ref[...]
    qm = qm_ref[...]
    ss = []
    for r in range(pp):
        lf = clf_refs[r][0, 0]
        suf = _lane_suffix_sum(lf)
        ss.append(_bdot_nt(qm, ck_refs[r][0, 0]) * scale + (decay + (suf - lf)))
        decay = decay + suf[:, 0:1]
    d_ref[...] = decay
    m_old = m_ref[...]
    m_new = m_old
    for s in ss:
        m_new = jnp.maximum(m_new, jnp.max(s, axis=1, keepdims=True))
    a = jnp.exp(m_old - m_new)
    l_new = a * l_ref[...]
    acc = a * acc_ref[...]
    for r in range(pp):
        p = jnp.exp(ss[r] - m_new)
        l_new = l_new + jnp.sum(p, axis=1, keepdims=True)
        acc = acc + _bdot(p, cv_refs[r][0, 0])
    l_ref[...] = l_new
    acc_ref[...] = acc
    m_ref[...] = m_new

    @pl.when(j == pl.num_programs(1) - 1)
    def _():
        o = acc_ref[...] / l_ref[...]
        o_ref[0] = jnp.sum(jnp.where(own, o, 0.0), axis=0, keepdims=True)


def _fox_sample(q, k_new, v_new, lf_new, cache_k, cache_v, cache_lf_t, page_table, layer):
    n, d_att = q.shape
    n_pages = page_table.shape[1]
    pp = math.gcd(n_pages, DECODE_PAGES_PER_STEP)
    per_seq = lambda shp: pl.BlockSpec((1,) + shp, lambda b, j, pt: (b, 0, 0))

    def paged(shp, r):
        return pl.BlockSpec((1, 1) + shp, lambda b, j, pt: (pt[b, n_pages - 1 - (j * pp + r)], layer, 0, 0))

    grid_spec = pltpu.PrefetchScalarGridSpec(
        num_scalar_prefetch=1,
        grid=(n, n_pages // pp),
        in_specs=([per_seq((1, d_att)), per_seq((1, d_att)), per_seq((1, d_att)), per_seq((1, N_HEADS))]
                  + [paged((PAGE, d_att), r) for r in range(pp)] + [paged((PAGE, d_att), r) for r in range(pp)]
                  + [paged((N_HEADS, PAGE), r) for r in range(pp)]),
        out_specs=per_seq((1, d_att)),
        scratch_shapes=[pltpu.VMEM((N_HEADS, d_att), jnp.float32), pltpu.VMEM((N_HEADS, 1), jnp.float32),
                        pltpu.VMEM((N_HEADS, 1), jnp.float32), pltpu.VMEM((N_HEADS, d_att), jnp.float32),
                        pltpu.VMEM((N_HEADS, 1), jnp.float32)],
    )
    o = pl.pallas_call(
        functools.partial(_fox_sample_kernel, scale=HEAD_DIM ** -0.5, pp=pp),
        grid_spec=grid_spec,
        out_shape=jax.ShapeDtypeStruct((n, 1, d_att), jnp.float32),
        compiler_params=_params("arbitrary", "arbitrary"),
        name="fox_sample",
    )(page_table, q.reshape(n, 1, d_att), k_new.reshape(n, 1, d_att), v_new.reshape(n, 1, d_att),
      lf_new.reshape(n, 1, N_HEADS), *([cache_k] * pp), *([cache_v] * pp), *([cache_lf_t] * pp))
    return o.reshape(n, d_att)


def _sg_prompt_kernel(u_ref, v_ref, g_ref, b_ref, ws_ref, bs_ref, o_ref, *, tm):
    v = _layer_norm(_gelu(v_ref[...]), g_ref[...], b_ref[...])
    tri = (lax.broadcasted_iota(jnp.int32, (CHUNK, CHUNK), 1)
           <= lax.broadcasted_iota(jnp.int32, (CHUNK, CHUNK), 0))
    gd = v.shape[1] // SG_GROUPS
    for g in range(SG_GROUPS):
        w = jnp.where(tri, ws_ref[g], 0.0)
        bias = bs_ref[:, g:g + 1]
        cols = slice(g * gd, (g + 1) * gd)
        for c in range(tm // CHUNK):
            rows = slice(c * CHUNK, (c + 1) * CHUNK)
            y = _bdot(w, v[rows, cols]) + bias
            o_ref[rows, cols] = _gelu(u_ref[rows, cols]) * y


def _sg_prompt(hw, ln_g, ln_b, w_s, b_s_t):
    n, d2 = hw.shape
    d = d2 // 2
    tm = 512
    fixed = lambda shp: pl.BlockSpec(shp, lambda i: (0,) * len(shp))
    return pl.pallas_call(
        functools.partial(_sg_prompt_kernel, tm=tm),
        grid=(n // tm,),
        in_specs=[pl.BlockSpec((tm, d), lambda i: (i, 0)), pl.BlockSpec((tm, d), lambda i: (i, 1)),
                  fixed((1, d)), fixed((1, d)), fixed(w_s.shape), fixed(b_s_t.shape)],
        out_specs=pl.BlockSpec((tm, d), lambda i: (i, 0)),
        out_shape=jax.ShapeDtypeStruct((n, d), jnp.float32),
        compiler_params=_params("arbitrary"),
        name="sg_prompt",
    )(hw, hw, ln_g.reshape(1, d), ln_b.reshape(1, d), w_s, b_s_t)


def _sg_sample_kernel(u_ref, v_ref, g_ref, b_ref, wd_ref, b0_ref, o_ref, vo_ref):
    v = _layer_norm(_gelu(v_ref[...]), g_ref[...], b_ref[...])
    vo_ref[...] = v
    o_ref[...] = _gelu(u_ref[...]) * (wd_ref[...] * v + b0_ref[...])


def _sg_sample(hw, ln_g, ln_b, w_diag0, b0):
    n, d2 = hw.shape
    d = d2 // 2
    fixed = lambda shp: pl.BlockSpec(shp, lambda i: (0,) * len(shp))
    return pl.pallas_call(
        _sg_sample_kernel,
        grid=(1,),
        in_specs=[pl.BlockSpec((n, d), lambda i: (0, 0)), pl.BlockSpec((n, d), lambda i: (0, 1)),
                  fixed((1, d)), fixed((1, d)), fixed((1, d)), fixed((1, d))],
        out_specs=[fixed((n, d)), fixed((n, d))],
        out_shape=[jax.ShapeDtypeStruct((n, d), jnp.float32), jax.ShapeDtypeStruct((n, d), jnp.float32)],
        compiler_params=_params("arbitrary"),
        name="sg_sample",
    )(hw, hw, ln_g.reshape(1, d), ln_b.reshape(1, d), w_diag0.reshape(1, d), b0.reshape(1, d))


def _top16(s, pos, ids):
    far = jnp.iinfo(jnp.int32).max
    vals, outs = [], []
    for _ in range(PEER_TOPK):
        m = jnp.max(s, axis=0, keepdims=True)
        am = jnp.min(jnp.where(s == m, pos, far), axis=0, keepdims=True)
        hit = pos == am
        vals.append(m)
        outs.append(am if ids is None else jnp.max(jnp.where(hit, ids, -1), axis=0, keepdims=True))
        s = jnp.where(hit, NEG_INF, s)
    return jnp.concatenate(vals, axis=0), jnp.concatenate(outs, axis=0)


def _peer_topk_kernel(q_ref, keys_ref, e_ref, g_ref):
    tb = q_ref.shape[0]
    key_pos = lax.broadcasted_iota(jnp.int32, (PEER_NKEYS, tb), 0)
    half = []
    for p in range(2):
        s = _bdot_nt(keys_ref[0, p], q_ref[:, p * PEER_NKEYS:(p + 1) * PEER_NKEYS])
        half.append(_top16(s, key_pos, None))
    (s0, i0), (s1, i1) = half
    row8 = lax.broadcasted_iota(jnp.int32, (SUBLANES, tb), 0)
    row16 = lax.broadcasted_iota(jnp.int32, (PEER_TOPK, tb), 0)
    cs, ci, cp = [s0[0:1, :] + s1], [i0[0:1, :] * PEER_NKEYS + i1], [row16]
    for a in range(1, SUBLANES):
        keep = row8 < PEER_TOPK // (a + 1)
        cs.append(jnp.where(keep, s0[a:a + 1, :] + s1[0:SUBLANES, :], NEG_INF))
        ci.append(i0[a:a + 1, :] * PEER_NKEYS + i1[0:SUBLANES, :])
        cp.append(jnp.where(keep, a * PEER_TOPK + row8, PEER_TOPK * PEER_TOPK + a * PEER_TOPK + row8))
    cs.append(s0[SUBLANES:, :] + s1[0:1, :])
    ci.append(i0[SUBLANES:, :] * PEER_NKEYS + i1[0:1, :])
    cp.append((row8 + SUBLANES) * PEER_TOPK)
    best_s, experts = _top16(jnp.concatenate(cs, axis=0), jnp.concatenate(cp, axis=0), jnp.concatenate(ci, axis=0))
    e = jnp.exp(best_s - best_s[0:1, :])
    e_ref[0] = experts
    g_ref[0] = e / jnp.sum(e, axis=0, keepdims=True)


def _peer_topk(q, keys):
    n = q.shape[0]
    tb = _row_block(n, 1024)
    dk = 2 * PEER_NKEYS
    out = pl.BlockSpec((1, PEER_TOPK, tb), lambda i, h: (h, 0, i))
    e, g = pl.pallas_call(
        _peer_topk_kernel,
        grid=(n // tb, PEER_HEADS),
        in_specs=[pl.BlockSpec((tb, dk), lambda i, h: (i, h)),
                  pl.BlockSpec((1, 2, PEER_NKEYS, dk // 2), lambda i, h: (h, 0, 0, 0))],
        out_specs=[out, out],
        out_shape=[jax.ShapeDtypeStruct((PEER_HEADS, PEER_TOPK, n), jnp.int32),
                   jax.ShapeDtypeStruct((PEER_HEADS, PEER_TOPK, n), jnp.float32)],
        compiler_params=_params("arbitrary", "arbitrary"),
        name="peer_topk",
    )(q, keys)
    return e.reshape(PEER_SLOTS, n), g.reshape(PEER_SLOTS, n)


def _tree_sum(xs):
    while len(xs) > 1:
        xs = [xs[k] + xs[k + 1] for k in range(0, len(xs) - 1, 2)] + ([xs[-1]] if len(xs) % 2 else [])
    return xs[0]


def _tile_mean(y):
    s = jnp.sum(jnp.sum(y, axis=-1, keepdims=True), axis=-2, keepdims=True)
    return s * (1.0 / (y.shape[-1] * y.shape[-2]))


def _fold_sublanes(a, b, r):
    low = (lax.broadcasted_iota(jnp.int32, a.shape, 0) % (2 * r)) < r
    return jnp.where(low, a, pltpu.roll(b, r, 0)) + jnp.where(low, pltpu.roll(a, SUBLANES - r, 0), b)


_FOLD_ORDER = (0, 4, 2, 6, 1, 5, 3, 7)


def _sublane_sums(tiles):
    v = [tiles[_FOLD_ORDER.index(k)] for k in range(SUBLANES)]
    r = SUBLANES // 2
    while len(v) > 1:
        v = [_fold_sublanes(v[2 * k], v[2 * k + 1], r) for k in range(len(v) // 2)]
        r //= 2
    return v[0]


def _peer_gather_kernel(idx_ref, x_ref, g_ref, lg_ref, lb_ref, uv_ref, o_ref, buf0_ref, buf1_ref, sem_ref,
                        *, tb, d, alpha):
    i = pl.program_id(0)
    n_groups = tb // GATHER_GROUP
    group_picks = GATHER_GROUP * PEER_SLOTS
    n_lane_tiles = d // LANES
    bufs = (buf0_ref, buf1_ref)

    def start(k, slot, t, row, thread=0):
        pltpu.make_async_copy(uv_ref.at[idx_ref[k]], bufs[slot].at[row], sem_ref.at[slot, t]).start(priority=thread)

    def wait(slot, t):
        rows = bufs[slot].at[pl.ds(t * PEER_SLOTS, PEER_SLOTS)]
        pltpu.make_async_copy(rows, rows, sem_ref.at[slot, t]).wait()

    def mix_token(tok, t, slot, pick0, slot_next):
        buf = bufs[slot]
        x = x_ref[tok]
        lane = lax.broadcasted_iota(jnp.int32, (SUBLANES, tb), 1)
        ys = []
        tok_row0 = t * PEER_SLOTS
        tok_pick0 = pick0 + tok_row0
        for r in range(PEER_SLOTS // SUBLANES):
            row0 = tok_row0 + r * SUBLANES
            for jj in range(SUBLANES):
                start(tok_pick0 + (r * SUBLANES + jj), slot_next, t, row0 + jj, thread=jj % DMA_THREADS)
            words = [buf[row0 + jj] for jj in range(SUBLANES)]
            us = [lax.bitcast_convert_type(wd << 16, jnp.float32) for wd in words]
            vs = [lax.bitcast_convert_type(wd & jnp.uint32(0xFFFF0000), jnp.float32) for wd in words]
            h = jnp.sum(_sublane_sums([u * x for u in us]), axis=1, keepdims=True)
            gate = jnp.sum(jnp.where(lane == tok, g_ref[0, r * SUBLANES:(r + 1) * SUBLANES, :], 0.0),
                           axis=1, keepdims=True)
            w = jnp.broadcast_to(gate * _gelu(h), (SUBLANES, LANES))
            ys.append(_tree_sum([jnp.broadcast_to(w[jj:jj + 1, :], (SUBLANES, LANES)) * vs[jj]
                                 for jj in range(SUBLANES)]))
        o_ref[tok] = alpha * x + _tree_sum(ys)

    def mix_group(grp, slot, slot_next):
        def body(tt, c):
            for dt in range(TOKENS_PER_ITER):
                wait(slot, tt * TOKENS_PER_ITER + dt)
            for dt in range(TOKENS_PER_ITER):
                t = tt * TOKENS_PER_ITER + dt
                mix_token(grp * GATHER_GROUP + t, t, slot, (grp + 1) * group_picks, slot_next)
            return c
        lax.fori_loop(0, GATHER_GROUP // TOKENS_PER_ITER, body, 0)

    @pl.when(i == 0)
    def _():
        def body(k, c):
            start(k, 0, k // PEER_SLOTS, k)
            return c
        lax.fori_loop(0, group_picks, body, 0, unroll=8)

    def pair(gg, c):
        mix_group(2 * gg, 0, 1)
        mix_group(2 * gg + 1, 1, 0)
        return c
    lax.fori_loop(0, n_groups // 2, pair, 0)

    @pl.when(i == pl.num_programs(0) - 1)
    def _():
        for t in range(GATHER_GROUP):
            wait(0, t)

    y = o_ref[...]
    yc = y - _tile_mean(y)
    o_ref[...] = yc * lax.rsqrt(_tile_mean(yc * yc) + LN_EPS) * lg_ref[...] + lb_ref[...]


def _pack_kernel(lo_ref, hi_ref, o_ref):
    bits = lambda r: lax.bitcast_convert_type(r[...].astype(jnp.bfloat16).astype(jnp.float32), jnp.uint32)
    packed = (bits(lo_ref) >> 16) | (bits(hi_ref) & jnp.uint32(0xFFFF0000))
    for c in range(o_ref.shape[1]):
        o_ref[:, c, :] = packed[:, c * LANES:(c + 1) * LANES]


def _pack_bf16_pair(lo, hi):
    n, d = lo.shape
    tm = 512
    return pl.pallas_call(
        _pack_kernel,
        grid=(n // tm,),
        in_specs=[pl.BlockSpec((tm, d), lambda i: (i, 0)), pl.BlockSpec((tm, d), lambda i: (i, 0))],
        out_specs=pl.BlockSpec((tm, d // LANES, LANES), lambda i: (i, 0, 0)),
        out_shape=jax.ShapeDtypeStruct((n, d // LANES, LANES), jnp.uint32),
        compiler_params=_params("arbitrary"),
        name="pack_tables",
    )(lo, hi)


def _peer_gather_ln(x, experts, gates, uv, ln_g, ln_b, alpha):
    n, d = x.shape
    tb = _row_block(n, GATHER_BLOCK)
    nblk = n // tb
    assert (tb // GATHER_GROUP) % 2 == 0 and d == SUBLANES * LANES
    group_picks = GATHER_GROUP * PEER_SLOTS
    picks = experts.T.reshape(nblk, tb * PEER_SLOTS)
    picks = jnp.concatenate([picks, jnp.roll(picks[:, :group_picks], -1, axis=0)], axis=1).reshape(-1)
    buf = pltpu.VMEM((group_picks, SUBLANES, LANES), jnp.uint32)
    tile = (SUBLANES, LANES)
    out = pl.pallas_call(
        functools.partial(_peer_gather_kernel, tb=tb, d=d, alpha=alpha),
        grid=(nblk,),
        in_specs=[pl.BlockSpec((tb * PEER_SLOTS + group_picks,), lambda i: (i,), memory_space=pltpu.SMEM),
                  pl.BlockSpec((tb,) + tile, lambda i: (i, 0, 0)),
                  pl.BlockSpec((1, PEER_SLOTS, tb), lambda i: (i, 0, 0)),
                  pl.BlockSpec(tile, lambda i: (0, 0)), pl.BlockSpec(tile, lambda i: (0, 0)),
                  pl.BlockSpec(memory_space=pl.ANY)],
        out_specs=pl.BlockSpec((tb,) + tile, lambda i: (i, 0, 0)),
        out_shape=jax.ShapeDtypeStruct((n,) + tile, jnp.float32),
        scratch_shapes=[buf, buf, pltpu.SemaphoreType.DMA((2, GATHER_GROUP))],
        compiler_params=_params("arbitrary", disable_bounds_checks=True),
        name="peer_gather",
    )(picks, x.reshape((n,) + tile), jnp.swapaxes(gates.reshape(PEER_SLOTS, nblk, tb), 0, 1),
      ln_g.reshape(tile), ln_b.reshape(tile), uv)
    return out.reshape(n, d)


def kernel(x_prompt, x_sample, cache_k, cache_v, cache_logf, state_conv, page_table, w_in_ab, b_f, conv_w, w_o_ab,
           w_in_c, sg_ln_g, sg_ln_b, w_s, b_s, w_o_c, ln1_g, ln1_b, ln2_g, ln2_b, peer_wq, peer_keys, peer_u, peer_v):
    n_seq, t, d = x_prompt.shape
    n_dec = x_sample.shape[0]
    depth = ln1_g.shape[0]
    d_conv = conv_w.shape[2]
    d_att = N_HEADS * HEAD_DIM
    d_in = w_in_ab.shape[2]
    d_in_pad = -(-d_in // (5 * LANES)) * (5 * LANES)
    alpha = (2.0 * depth) ** 0.25
    bf16 = jnp.bfloat16

    xp = x_prompt.reshape(n_seq * t, d)
    xs = x_sample.reshape(n_dec, d)
    n_phys = cache_k.shape[0]
    ck = cache_k.reshape(n_phys, -1, PAGE, d_att)
    cv = cache_v.reshape(n_phys, -1, PAGE, d_att)
    clf_t = jnp.swapaxes(cache_logf, 2, 3)

    kp_l, vp_l, fp_l, cp_l, ks_l, vs_l, fs_l, cs_l, sv_l = [], [], [], [], [], [], [], [], []
    for l in range(depth):
        i = l // 2
        if l % 2 == 0:
            w_in = jnp.pad(w_in_ab[i], ((0, 0), (0, d_in_pad - d_in))).astype(bf16)
            w_o = w_o_ab[i].astype(bf16)
            xw = _matmul(xp, w_in, d_in_pad // 5)
            a_out, lf, conv_new = _conv_prompt(xw, b_f[i], conv_w[i], n_seq, d_conv)
            cum_t = _cumsum_time(jnp.swapaxes(lf.reshape(n_seq, t, N_HEADS), 1, 2))
            cum = jnp.swapaxes(cum_t, 1, 2).reshape(n_seq * t, N_HEADS)
            o = _fox_prompt(xw, cum, cum_t, n_seq, d_conv)
            kp_l.append(xw[:, 3 * d_conv + d_att:3 * d_conv + 2 * d_att].reshape(n_seq, t, N_HEADS, HEAD_DIM))
            vp_l.append(xw[:, 3 * d_conv + 2 * d_att:3 * d_conv + 3 * d_att].reshape(n_seq, t, N_HEADS, HEAD_DIM))
            fp_l.append(lf.reshape(n_seq, t, N_HEADS))
            cp_l.append(conv_new)
            xp = _matmul_res_ln([a_out, o], [w_o[:d_conv], w_o[d_conv:]], xp, ln1_g[l], ln1_b[l], alpha)
            xw = _matmul(xs, w_in, d_in_pad // 5)
            a_out, lf, z = _conv_sample(xw, state_conv[:, i, 0], state_conv[:, i, 1], b_f[i], conv_w[i], d_conv)
            q = xw[:, 3 * d_conv:3 * d_conv + d_att]
            k = xw[:, 3 * d_conv + d_att:3 * d_conv + 2 * d_att]
            v = xw[:, 3 * d_conv + 2 * d_att:3 * d_conv + 3 * d_att]
            o = _fox_sample(q, k, v, lf, ck, cv, clf_t, page_table, i)
            ks_l.append(k.reshape(n_dec, 1, N_HEADS, HEAD_DIM))
            vs_l.append(v.reshape(n_dec, 1, N_HEADS, HEAD_DIM))
            fs_l.append(lf.reshape(n_dec, 1, N_HEADS))
            cs_l.append(jnp.stack([state_conv[:, i, 1], z], axis=1))
            xs = _matmul_res_ln([a_out, o], [w_o[:d_conv], w_o[d_conv:]], xs, ln1_g[l], ln1_b[l], alpha)
        else:
            w_in = w_in_c[i].astype(bf16)
            w_o = w_o_c[i].astype(bf16)
            d_sg = w_o.shape[0]
            hw = _matmul(xp, w_in, d_sg)
            sg = _sg_prompt(hw, sg_ln_g[i], sg_ln_b[i], w_s[i], b_s[i].T)
            xp = _matmul_res_ln([sg], [w_o], xp, ln1_g[l], ln1_b[l], alpha)
            hw = _matmul(xs, w_in, d_sg)
            gd = d_sg // SG_GROUPS
            sg, sv = _sg_sample(hw, sg_ln_g[i], sg_ln_b[i], jnp.repeat(w_s[i, :, 0, 0], gd), jnp.repeat(b_s[i, :, 0], gd))
            sv_l.append(sv.reshape(n_dec, 1, d_sg))
            xs = _matmul_res_ln([sg], [w_o], xs, ln1_g[l], ln1_b[l], alpha)
        wq = peer_wq[l].astype(bf16)
        uv = _pack_bf16_pair(peer_u[l], peer_v[l])
        for which in range(2):
            xx = xp if which == 0 else xs
            qq = _matmul(xx, wq, wq.shape[1] // 2)
            experts, gates = _peer_topk(qq, peer_keys[l])
            xx = _peer_gather_ln(xx, experts, gates, uv, ln2_g[l], ln2_b[l], alpha)
            if which == 0:
                xp = xx
            else:
                xs = xx

    return (xp.reshape(n_seq, t, d), xs.reshape(n_dec, 1, d),
            jnp.stack(kp_l, axis=1), jnp.stack(vp_l, axis=1), jnp.stack(fp_l, axis=1),
            jnp.stack(ks_l, axis=1), jnp.stack(vs_l, axis=1), jnp.stack(fs_l, axis=1),
            jnp.stack(cp_l, axis=1), jnp.stack(cs_l, axis=1), jnp.stack(sv_l, axis=1))
```

```python
import functools
import math

import jax
import jax.numpy as jnp
from jax import lax
from jax.experimental import pallas as pl
from jax.experimental.pallas import tpu as pltpu

LANES = 128
SUBLANES = 8
VMEM_LIMIT = 48 * 1024 * 1024

LN_EPS = 1e-5
HEAD_DIM = 64
N_HEADS = 8
PAGE = 128
CHUNK = 128
SG_GROUPS = 8
PEER_HEADS = 8
PEER_NKEYS = 128
PEER_TOPK = 16
PEER_SLOTS = PEER_HEADS * PEER_TOPK
GATHER_GROUP = 8
DMA_THREADS = 2
GATHER_BLOCK = 64
TOKENS_PER_ITER = 4
FOX_ROW_SPLIT = 2
DECODE_PAGES_PER_STEP = 8
NEG_INF = float("-inf")


def _params(*sem, **kw):
    return pltpu.CompilerParams(dimension_semantics=sem, vmem_limit_bytes=VMEM_LIMIT, **kw)


def _gelu(x):
    return 0.5 * x * (1.0 + jnp.tanh(math.sqrt(2.0 / math.pi) * (x + 0.044715 * (x * x * x))))


def _layer_norm(x, g, b):
    mu = jnp.mean(x, axis=-1, keepdims=True)
    xc = x - mu
    var = jnp.mean(xc * xc, axis=-1, keepdims=True)
    return xc * lax.rsqrt(var + LN_EPS) * g + b


def _log_sigmoid(x):
    return -(jnp.maximum(-x, 0.0) + jnp.log1p(jnp.exp(-jnp.abs(x))))


def _bdot(a, b):
    return jnp.dot(a.astype(jnp.bfloat16), b.astype(jnp.bfloat16), preferred_element_type=jnp.float32)


def _bdot_nt(a, b):
    return lax.dot_general(a.astype(jnp.bfloat16), b.astype(jnp.bfloat16), (((1,), (1,)), ((), ())),
                           preferred_element_type=jnp.float32)


def _row_block(n, pref):
    return pref if n % pref == 0 else n


def _mm_kernel(x_ref, w_ref, o_ref):
    o_ref[...] = _bdot(x_ref[...], w_ref[...])


def _matmul(x, w, tn):
    n, k = x.shape
    m = w.shape[1]
    tm = _row_block(n, 512)
    return pl.pallas_call(
        _mm_kernel,
        grid=(m // tn, n // tm),
        in_specs=[pl.BlockSpec((tm, k), lambda j, i: (i, 0)),
                  pl.BlockSpec((k, tn), lambda j, i: (0, j))],
        out_specs=pl.BlockSpec((tm, tn), lambda j, i: (i, j)),
        out_shape=jax.ShapeDtypeStruct((n, m), jnp.float32),
        compiler_params=_params("arbitrary", "arbitrary"),
        name="matmul",
    )(x, w)


def _mm_res_ln_kernel(*refs, n_in, alpha):
    xs, ws = refs[:n_in], refs[n_in:2 * n_in]
    res_ref, g_ref, b_ref, o_ref = refs[2 * n_in:]
    y = alpha * res_ref[...]
    for x_ref, w_ref in zip(xs, ws):
        y = y + _bdot(x_ref[...], w_ref[...])
    o_ref[...] = _layer_norm(y, g_ref[...], b_ref[...])


def _matmul_res_ln(xs, ws, res, g, b, alpha):
    n, d = res.shape
    tm = _row_block(n, 512)
    row = lambda i: (i, 0)
    fixed = lambda i: (0, 0)
    in_specs = ([pl.BlockSpec((tm, x.shape[1]), row) for x in xs]
                + [pl.BlockSpec(w.shape, fixed) for w in ws]
                + [pl.BlockSpec((tm, d), row), pl.BlockSpec((1, d), fixed), pl.BlockSpec((1, d), fixed)])
    return pl.pallas_call(
        functools.partial(_mm_res_ln_kernel, n_in=len(xs), alpha=alpha),
        grid=(n // tm,),
        in_specs=in_specs,
        out_specs=pl.BlockSpec((tm, d), row),
        out_shape=jax.ShapeDtypeStruct((n, d), jnp.float32),
        compiler_params=_params("arbitrary"),
        name="matmul_res_ln",
    )(*xs, *ws, res, g.reshape(1, d), b.reshape(1, d))


def _conv_prompt_kernel(gb_ref, gc_ref, hc_ref, gcp_ref, hcp_ref, fl_ref, bf_ref, cw_ref,
                        a_ref, lf_ref, cn_ref, zs_ref, *, tm, blocks_per_seq):
    i = pl.program_id(0)
    z = gc_ref[...] * hc_ref[...]
    zprev = gcp_ref[...] * hcp_ref[...]
    zprev = jnp.where(i % blocks_per_seq == 0, jnp.zeros_like(zprev), zprev)
    zs_ref[0:SUBLANES, :] = zprev
    zs_ref[SUBLANES:, :] = z
    z1 = zs_ref[pl.ds(SUBLANES - 1, tm), :]
    z2 = zs_ref[pl.ds(SUBLANES - 2, tm), :]
    cw = cw_ref[...]
    y = cw[0:1, :] * z2 + cw[1:2, :] * z1 + cw[2:3, :] * z
    a_ref[...] = gb_ref[...] * y
    lf_ref[...] = _log_sigmoid(fl_ref[:, 0:N_HEADS] + bf_ref[...])
    cn_ref[0] = zs_ref[pl.ds(tm + SUBLANES - 2, 2), :]


def _conv_prompt(xw, b_f, conv_w, n_seq, d_conv):
    n = xw.shape[0]
    t = n // n_seq
    tm = 512
    bps = t // tm
    fl_blk = (6 * d_conv) // LANES
    col = lambda c: (lambda i: (i, c))
    prev = lambda c: (lambda i: (jnp.maximum(i * (tm // SUBLANES) - 1, 0), c))
    return pl.pallas_call(
        functools.partial(_conv_prompt_kernel, tm=tm, blocks_per_seq=bps),
        grid=(n // tm,),
        in_specs=[pl.BlockSpec((tm, d_conv), col(0)), pl.BlockSpec((tm, d_conv), col(1)),
                  pl.BlockSpec((tm, d_conv), col(2)),
                  pl.BlockSpec((SUBLANES, d_conv), prev(1)), pl.BlockSpec((SUBLANES, d_conv), prev(2)),
                  pl.BlockSpec((tm, LANES), col(fl_blk)),
                  pl.BlockSpec((1, N_HEADS), lambda i: (0, 0)),
                  pl.BlockSpec(conv_w.shape, lambda i: (0, 0))],
        out_specs=[pl.BlockSpec((tm, d_conv), lambda i: (i, 0)),
                   pl.BlockSpec((tm, N_HEADS), lambda i: (i, 0)),
                   pl.BlockSpec((1, 2, d_conv), lambda i: (i // bps, 0, 0))],
        out_shape=[jax.ShapeDtypeStruct((n, d_conv), jnp.float32),
                   jax.ShapeDtypeStruct((n, N_HEADS), jnp.float32),
                   jax.ShapeDtypeStruct((n_seq, 2, d_conv), jnp.float32)],
        scratch_shapes=[pltpu.VMEM((tm + SUBLANES, d_conv), jnp.float32)],
        compiler_params=_params("arbitrary"),
        name="conv_prompt",
    )(xw, xw, xw, xw, xw, xw, b_f.reshape(1, N_HEADS), conv_w)


def _conv_sample_kernel(gb_ref, gc_ref, hc_ref, fl_ref, p0_ref, p1_ref, bf_ref, cw_ref, a_ref, lf_ref, z_ref):
    z = gc_ref[...] * hc_ref[...]
    cw = cw_ref[...]
    y = cw[0:1, :] * p0_ref[...] + cw[1:2, :] * p1_ref[...] + cw[2:3, :] * z
    a_ref[...] = gb_ref[...] * y
    lf_ref[...] = _log_sigmoid(fl_ref[:, 0:N_HEADS] + bf_ref[...])
    z_ref[...] = z


def _conv_sample(xw, prev0, prev1, b_f, conv_w, d_conv):
    n = xw.shape[0]
    fl_blk = (6 * d_conv) // LANES
    full = lambda shp: pl.BlockSpec(shp, lambda i: (0,) * len(shp))
    return pl.pallas_call(
        _conv_sample_kernel,
        grid=(1,),
        in_specs=[pl.BlockSpec((n, d_conv), lambda i: (0, 0)), pl.BlockSpec((n, d_conv), lambda i: (0, 1)),
                  pl.BlockSpec((n, d_conv), lambda i: (0, 2)), pl.BlockSpec((n, LANES), lambda i: (0, fl_blk)),
                  full((n, d_conv)), full((n, d_conv)), full((1, N_HEADS)), full(conv_w.shape)],
        out_specs=[full((n, d_conv)), full((n, N_HEADS)), full((n, d_conv))],
        out_shape=[jax.ShapeDtypeStruct((n, d_conv), jnp.float32),
                   jax.ShapeDtypeStruct((n, N_HEADS), jnp.float32),
                   jax.ShapeDtypeStruct((n, d_conv), jnp.float32)],
        compiler_params=_params("arbitrary"),
        name="conv_sample",
    )(xw, xw, xw, xw, prev0, prev1, b_f.reshape(1, N_HEADS), conv_w)


def _lane_prefix_sum(x):
    lane = lax.broadcasted_iota(jnp.int32, x.shape, 1)
    d = 1
    while d < LANES:
        x = x + jnp.where(lane >= d, pltpu.roll(x, d, 1), 0.0)
        d *= 2
    return x


def _lane_suffix_sum(x):
    lane = lax.broadcasted_iota(jnp.int32, x.shape, 1)
    d = 1
    while d < LANES:
        x = x + jnp.where(lane < LANES - d, pltpu.roll(x, LANES - d, 1), 0.0)
        d *= 2
    return x


def _cumsum_kernel(x_ref, o_ref, *, t):
    carry = jnp.zeros((N_HEADS, 1), jnp.float32)
    for c in range(t // LANES):
        sl = slice(c * LANES, (c + 1) * LANES)
        s = _lane_prefix_sum(x_ref[0, :, sl]) + carry
        o_ref[0, :, sl] = s
        carry = s[:, LANES - 1:LANES]


def _cumsum_time(lf_t):
    n_seq, h, t = lf_t.shape
    return pl.pallas_call(
        functools.partial(_cumsum_kernel, t=t),
        grid=(n_seq,),
        in_specs=[pl.BlockSpec((1, h, t), lambda b: (b, 0, 0))],
        out_specs=pl.BlockSpec((1, h, t), lambda b: (b, 0, 0)),
        out_shape=jax.ShapeDtypeStruct(lf_t.shape, jnp.float32),
        compiler_params=_params("arbitrary"),
        name="cumsum_time",
    )(lf_t)


def _fox_prompt_kernel(q_ref, k_ref, v_ref, cq_ref, ck_ref, o_ref, m_ref, l_ref, acc_ref, *, tq, scale):
    qb, kb = pl.program_id(1), pl.program_id(2)

    @pl.when(kb == 0)
    def _():
        m_ref[...] = jnp.full(m_ref.shape, NEG_INF, jnp.float32)
        l_ref[...] = jnp.zeros(l_ref.shape, jnp.float32)
        acc_ref[...] = jnp.zeros(acc_ref.shape, jnp.float32)

    @pl.when(kb <= qb)
    def _():
        tr = tq // FOX_ROW_SPLIT
        lane = lax.broadcasted_iota(jnp.int32, (1, LANES), 1)
        for part in range(FOX_ROW_SPLIT):
            rows = slice(part * tr, (part + 1) * tr)
            row = qb * tq + part * tr + lax.broadcasted_iota(jnp.int32, (tr, tq), 0)
            colp = kb * tq + lax.broadcasted_iota(jnp.int32, (tr, tq), 1)
            causal = colp <= row
            for h in range(N_HEADS):
                pair = slice((h // 2) * LANES, (h // 2 + 1) * LANES)
                own = (lane // HEAD_DIM) == (h % 2)
                qh = jnp.where(own, q_ref[rows, pair], 0.0)
                s = _bdot_nt(qh, k_ref[:, pair]) * scale
                s = s + cq_ref[rows, h:h + 1] - ck_ref[0, h:h + 1, :]
                s = jnp.where(causal, s, NEG_INF)
                m_old = m_ref[h, rows]
                m_new = jnp.maximum(m_old, jnp.max(s, axis=1, keepdims=True))
                p = jnp.exp(s - m_new)
                a = jnp.exp(m_old - m_new)
                l_ref[h, rows] = a * l_ref[h, rows] + jnp.sum(p, axis=1, keepdims=True)
                acc_ref[h, rows] = a * acc_ref[h, rows] + _bdot(p, v_ref[:, pair])
                m_ref[h, rows] = m_new

    @pl.when(kb == qb)
    def _():
        lane = lax.broadcasted_iota(jnp.int32, (1, LANES), 1)
        for hp in range(N_HEADS // 2):
            o0 = acc_ref[2 * hp] / l_ref[2 * hp]
            o1 = acc_ref[2 * hp + 1] / l_ref[2 * hp + 1]
            o_ref[:, hp * LANES:(hp + 1) * LANES] = jnp.where(lane < HEAD_DIM, o0, o1)


def _fox_prompt(xw, cum, cum_t, n_seq, d_conv):
    n = xw.shape[0]
    t = n // n_seq
    d_att = N_HEADS * HEAD_DIM
    tq = 512
    nb = t // tq
    qc, kc, vc = (3 * d_conv) // d_att, (3 * d_conv) // d_att + 1, (3 * d_conv) // d_att + 2
    return pl.pallas_call(
        functools.partial(_fox_prompt_kernel, tq=tq, scale=HEAD_DIM ** -0.5),
        grid=(n_seq, nb, nb),
        in_specs=[pl.BlockSpec((tq, d_att), lambda b, i, j: (b * nb + i, qc)),
                  pl.BlockSpec((tq, d_att), lambda b, i, j: (b * nb + jnp.minimum(i, j), kc)),
                  pl.BlockSpec((tq, d_att), lambda b, i, j: (b * nb + jnp.minimum(i, j), vc)),
                  pl.BlockSpec((tq, N_HEADS), lambda b, i, j: (b * nb + i, 0)),
                  pl.BlockSpec((1, N_HEADS, tq), lambda b, i, j: (b, 0, jnp.minimum(i, j)))],
        out_specs=pl.BlockSpec((tq, d_att), lambda b, i, j: (b * nb + i, 0)),
        out_shape=jax.ShapeDtypeStruct((n, d_att), jnp.float32),
        scratch_shapes=[pltpu.VMEM((N_HEADS, tq, 1), jnp.float32), pltpu.VMEM((N_HEADS, tq, 1), jnp.float32),
                        pltpu.VMEM((N_HEADS, tq, LANES), jnp.float32)],
        compiler_params=_params("arbitrary", "arbitrary", "arbitrary"),
        name="fox_prompt",
    )(xw, xw, xw, cum, cum_t)


def _fox_sample_kernel(pt_ref, q_ref, kn_ref, vn_ref, lfn_ref, *refs, scale, pp):
    ck_refs, cv_refs, clf_refs = refs[:pp], refs[pp:2 * pp], refs[2 * pp:3 * pp]
    o_ref, qm_ref, m_ref, l_ref, acc_ref, d_ref = refs[3 * pp:]
    j = pl.program_id(1)
    d_att = N_HEADS * HEAD_DIM
    head_of_lane = lax.broadcasted_iota(jnp.int32, (N_HEADS, d_att), 1) // HEAD_DIM
    head_of_row = lax.broadcasted_iota(jnp.int32, (N_HEADS, d_att), 0)
    own = head_of_lane == head_of_row

    @pl.when(j == 0)
    def _():
        qm = jnp.where(own, jnp.broadcast_to(q_ref[0], (N_HEADS, d_att)), 0.0)
        qm_ref[...] = qm
        m_ref[...] = jnp.sum(qm * kn_ref[0], axis=1, keepdims=True) * scale
        l_ref[...] = jnp.ones(l_ref.shape, jnp.float32)
        acc_ref[...] = jnp.broadcast_to(vn_ref[0], (N_HEADS, d_att))
        eye = (lax.broadcasted_iota(jnp.int32, (N_HEADS, N_HEADS), 0)
               == lax.broadcasted_iota(jnp.int32, (N_HEADS, N_HEADS), 1))
        d_ref[...] = jnp.sum(jnp.where(eye, jnp.broadcast_to(lfn_ref[0], (N_HEADS, N_HEADS)), 0.0),
                             axis=1, keepdims=True)

    decay = d_ref[...]
    qm = qm_ref[...]
    ss = []
    for r in range(pp):
        lf = clf_refs[r][0, 0]
        suf = _lane_suffix_sum(lf)
        ss.append(_bdot_nt(qm, ck_refs[r][0, 0]) * scale + (decay + (suf - lf)))
        decay = decay + suf[:, 0:1]
    d_ref[...] = decay
    m_old = m_ref[...]
    m_new = m_old
    for s in ss:
        m_new = jnp.maximum(m_new, jnp.max(s, axis=1, keepdims=True))
    a = jnp.exp(m_old - m_new)
    l_new = a * l_ref[...]
    acc = a * acc_ref[...]
    for r in range(pp):
        p = jnp.exp(ss[r] - m_new)
        l_new = l_new + jnp.sum(p, axis=1, keepdims=True)
        acc = acc + _bdot(p, cv_refs[r][0, 0])
    l_ref[...] = l_new
    acc_ref[...] = acc
    m_ref[...] = m_new

    @pl.when(j == pl.num_programs(1) - 1)
    def _():
        o = acc_ref[...] / l_ref[...]
        o_ref[0] = jnp.sum(jnp.where(own, o, 0.0), axis=0, keepdims=True)


def _fox_sample(q, k_new, v_new, lf_new, cache_k, cache_v, cache_lf_t, page_table, layer):
    n, d_att = q.shape
    n_pages = page_table.shape[1]
    pp = math.gcd(n_pages, DECODE_PAGES_PER_STEP)
    per_seq = lambda shp: pl.BlockSpec((1,) + shp, lambda b, j, pt: (b, 0, 0))

    def paged(shp, r):
        return pl.BlockSpec((1, 1) + shp, lambda b, j, pt: (pt[b, n_pages - 1 - (j * pp + r)], layer, 0, 0))

    grid_spec = pltpu.PrefetchScalarGridSpec(
        num_scalar_prefetch=1,
        grid=(n, n_pages // pp),
        in_specs=([per_seq((1, d_att)), per_seq((1, d_att)), per_seq((1, d_att)), per_seq((1, N_HEADS))]
                  + [paged((PAGE, d_att), r) for r in range(pp)] + [paged((PAGE, d_att), r) for r in range(pp)]
                  + [paged((N_HEADS, PAGE), r) for r in range(pp)]),
        out_specs=per_seq((1, d_att)),
        scratch_shapes=[pltpu.VMEM((N_HEADS, d_att), jnp.float32), pltpu.VMEM((N_HEADS, 1), jnp.float32),
                        pltpu.VMEM((N_HEADS, 1), jnp.float32), pltpu.VMEM((N_HEADS, d_att), jnp.float32),
                        pltpu.VMEM((N_HEADS, 1), jnp.float32)],
    )
    o = pl.pallas_call(
        functools.partial(_fox_sample_kernel, scale=HEAD_DIM ** -0.5, pp=pp),
        grid_spec=grid_spec,
        out_shape=jax.ShapeDtypeStruct((n, 1, d_att), jnp.float32),
        compiler_params=_params("arbitrary", "arbitrary"),
        name="fox_sample",
    )(page_table, q.reshape(n, 1, d_att), k_new.reshape(n, 1, d_att), v_new.reshape(n, 1, d_att),
      lf_new.reshape(n, 1, N_HEADS), *([cache_k] * pp), *([cache_v] * pp), *([cache_lf_t] * pp))
    return o.reshape(n, d_att)


def _sg_prompt_kernel(u_ref, v_ref, g_ref, b_ref, ws_ref, bs_ref, o_ref, *, tm):
    v = _layer_norm(_gelu(v_ref[...]), g_ref[...], b_ref[...])
    tri = (lax.broadcasted_iota(jnp.int32, (CHUNK, CHUNK), 1)
           <= lax.broadcasted_iota(jnp.int32, (CHUNK, CHUNK), 0))
    gd = v.shape[1] // SG_GROUPS
    for g in range(SG_GROUPS):
        w = jnp.where(tri, ws_ref[g], 0.0)
        bias = bs_ref[:, g:g + 1]
        cols = slice(g * gd, (g + 1) * gd)
        for c in range(tm // CHUNK):
            rows = slice(c * CHUNK, (c + 1) * CHUNK)
            y = _bdot(w, v[rows, cols]) + bias
            o_ref[rows, cols] = _gelu(u_ref[rows, cols]) * y


def _sg_prompt(hw, ln_g, ln_b, w_s, b_s_t):
    n, d2 = hw.shape
    d = d2 // 2
    tm = 512
    fixed = lambda shp: pl.BlockSpec(shp, lambda i: (0,) * len(shp))
    return pl.pallas_call(
        functools.partial(_sg_prompt_kernel, tm=tm),
        grid=(n // tm,),
        in_specs=[pl.BlockSpec((tm, d), lambda i: (i, 0)), pl.BlockSpec((tm, d), lambda i: (i, 1)),
                  fixed((1, d)), fixed((1, d)), fixed(w_s.shape), fixed(b_s_t.shape)],
        out_specs=pl.BlockSpec((tm, d), lambda i: (i, 0)),
        out_shape=jax.ShapeDtypeStruct((n, d), jnp.float32),
        compiler_params=_params("arbitrary"),
        name="sg_prompt",
    )(hw, hw, ln_g.reshape(1, d), ln_b.reshape(1, d), w_s, b_s_t)


def _sg_sample_kernel(u_ref, v_ref, g_ref, b_ref, wd_ref, b0_ref, o_ref, vo_ref):
    v = _layer_norm(_gelu(v_ref[...]), g_ref[...], b_ref[...])
    vo_ref[...] = v
    o_ref[...] = _gelu(u_ref[...]) * (wd_ref[...] * v + b0_ref[...])


def _sg_sample(hw, ln_g, ln_b, w_diag0, b0):
    n, d2 = hw.shape
    d = d2 // 2
    fixed = lambda shp: pl.BlockSpec(shp, lambda i: (0,) * len(shp))
    return pl.pallas_call(
        _sg_sample_kernel,
        grid=(1,),
        in_specs=[pl.BlockSpec((n, d), lambda i: (0, 0)), pl.BlockSpec((n, d), lambda i: (0, 1)),
                  fixed((1, d)), fixed((1, d)), fixed((1, d)), fixed((1, d))],
        out_specs=[fixed((n, d)), fixed((n, d))],
        out_shape=[jax.ShapeDtypeStruct((n, d), jnp.float32), jax.ShapeDtypeStruct((n, d), jnp.float32)],
        compiler_params=_params("arbitrary"),
        name="sg_sample",
    )(hw, hw, ln_g.reshape(1, d), ln_b.reshape(1, d), w_diag0.reshape(1, d), b0.reshape(1, d))


def _top16(s, pos, ids):
    far = jnp.iinfo(jnp.int32).max
    vals, outs = [], []
    for _ in range(PEER_TOPK):
        m = jnp.max(s, axis=0, keepdims=True)
        am = jnp.min(jnp.where(s == m, pos, far), axis=0, keepdims=True)
        hit = pos == am
        vals.append(m)
        outs.append(am if ids is None else jnp.max(jnp.where(hit, ids, -1), axis=0, keepdims=True))
        s = jnp.where(hit, NEG_INF, s)
    return jnp.concatenate(vals, axis=0), jnp.concatenate(outs, axis=0)


def _peer_topk_kernel(q_ref, keys_ref, e_ref, g_ref):
    tb = q_ref.shape[0]
    key_pos = lax.broadcasted_iota(jnp.int32, (PEER_NKEYS, tb), 0)
    half = []
    for p in range(2):
        s = _bdot_nt(keys_ref[0, p], q_ref[:, p * PEER_NKEYS:(p + 1) * PEER_NKEYS])
        half.append(_top16(s, key_pos, None))
    (s0, i0), (s1, i1) = half
    row8 = lax.broadcasted_iota(jnp.int32, (SUBLANES, tb), 0)
    row16 = lax.broadcasted_iota(jnp.int32, (PEER_TOPK, tb), 0)
    cs, ci, cp = [s0[0:1, :] + s1], [i0[0:1, :] * PEER_NKEYS + i1], [row16]
    for a in range(1, SUBLANES):
        keep = row8 < PEER_TOPK // (a + 1)
        cs.append(jnp.where(keep, s0[a:a + 1, :] + s1[0:SUBLANES, :], NEG_INF))
        ci.append(i0[a:a + 1, :] * PEER_NKEYS + i1[0:SUBLANES, :])
        cp.append(jnp.where(keep, a * PEER_TOPK + row8, PEER_TOPK * PEER_TOPK + a * PEER_TOPK + row8))
    cs.append(s0[SUBLANES:, :] + s1[0:1, :])
    ci.append(i0[SUBLANES:, :] * PEER_NKEYS + i1[0:1, :])
    cp.append((row8 + SUBLANES) * PEER_TOPK)
    best_s, experts = _top16(jnp.concatenate(cs, axis=0), jnp.concatenate(cp, axis=0), jnp.concatenate(ci, axis=0))
    e = jnp.exp(best_s - best_s[0:1, :])
    e_ref[0] = experts
    g_ref[0] = e / jnp.sum(e, axis=0, keepdims=True)


def _peer_topk(q, keys):
    n = q.shape[0]
    tb = _row_block(n, 1024)
    dk = 2 * PEER_NKEYS
    out = pl.BlockSpec((1, PEER_TOPK, tb), lambda i, h: (h, 0, i))
    e, g = pl.pallas_call(
        _peer_topk_kernel,
        grid=(n // tb, PEER_HEADS),
        in_specs=[pl.BlockSpec((tb, dk), lambda i, h: (i, h)),
                  pl.BlockSpec((1, 2, PEER_NKEYS, dk // 2), lambda i, h: (h, 0, 0, 0))],
        out_specs=[out, out],
        out_shape=[jax.ShapeDtypeStruct((PEER_HEADS, PEER_TOPK, n), jnp.int32),
                   jax.ShapeDtypeStruct((PEER_HEADS, PEER_TOPK, n), jnp.float32)],
        compiler_params=_params("arbitrary", "arbitrary"),
        name="peer_topk",
    )(q, keys)
    return e.reshape(PEER_SLOTS, n), g.reshape(PEER_SLOTS, n)


def _tree_sum(xs):
    while len(xs) > 1:
        xs = [xs[k] + xs[k + 1] for k in range(0, len(xs) - 1, 2)] + ([xs[-1]] if len(xs) % 2 else [])
    return xs[0]


def _tile_mean(y):
    s = jnp.sum(jnp.sum(y, axis=-1, keepdims=True), axis=-2, keepdims=True)
    return s * (1.0 / (y.shape[-1] * y.shape[-2]))


def _fold_sublanes(a, b, r):
    low = (lax.broadcasted_iota(jnp.int32, a.shape, 0) % (2 * r)) < r
    return jnp.where(low, a, pltpu.roll(b, r, 0)) + jnp.where(low, pltpu.roll(a, SUBLANES - r, 0), b)


_FOLD_ORDER = (0, 4, 2, 6, 1, 5, 3, 7)


def _sublane_sums(tiles):
    v = [tiles[_FOLD_ORDER.index(k)] for k in range(SUBLANES)]
    r = SUBLANES // 2
    while len(v) > 1:
        v = [_fold_sublanes(v[2 * k], v[2 * k + 1], r) for k in range(len(v) // 2)]
        r //= 2
    return v[0]


def _peer_gather_kernel(idx_ref, x_ref, g_ref, lg_ref, lb_ref, uv_ref, o_ref, buf0_ref, buf1_ref, sem_ref,
                        *, tb, d, alpha):
    i = pl.program_id(0)
    n_groups = tb // GATHER_GROUP
    group_picks = GATHER_GROUP * PEER_SLOTS
    n_lane_tiles = d // LANES
    bufs = (buf0_ref, buf1_ref)

    def start(k, slot, t, row, thread=0):
        pltpu.make_async_copy(uv_ref.at[idx_ref[k]], bufs[slot].at[row], sem_ref.at[slot, t]).start(priority=thread)

    def wait(slot, t):
        rows = bufs[slot].at[pl.ds(t * PEER_SLOTS, PEER_SLOTS)]
        pltpu.make_async_copy(rows, rows, sem_ref.at[slot, t]).wait()

    def mix_token(tok, t, slot, pick0, slot_next):
        buf = bufs[slot]
        x = x_ref[tok]
        lane = lax.broadcasted_iota(jnp.int32, (SUBLANES, tb), 1)
        ys = []
        tok_row0 = t * PEER_SLOTS
        tok_pick0 = pick0 + tok_row0
        for r in range(PEER_SLOTS // SUBLANES):
            row0 = tok_row0 + r * SUBLANES
            for jj in range(SUBLANES):
                start(tok_pick0 + (r * SUBLANES + jj), slot_next, t, row0 + jj, thread=jj % DMA_THREADS)
            words = [buf[row0 + jj] for jj in range(SUBLANES)]
            us = [lax.bitcast_convert_type(wd << 16, jnp.float32) for wd in words]
            vs = [lax.bitcast_convert_type(wd & jnp.uint32(0xFFFF0000), jnp.float32) for wd in words]
            h = jnp.sum(_sublane_sums([u * x for u in us]), axis=1, keepdims=True)
            gate = jnp.sum(jnp.where(lane == tok, g_ref[0, r * SUBLANES:(r + 1) * SUBLANES, :], 0.0),
                           axis=1, keepdims=True)
            w = jnp.broadcast_to(gate * _gelu(h), (SUBLANES, LANES))
            ys.append(_tree_sum([jnp.broadcast_to(w[jj:jj + 1, :], (SUBLANES, LANES)) * vs[jj]
                                 for jj in range(SUBLANES)]))
        o_ref[tok] = alpha * x + _tree_sum(ys)

    def mix_group(grp, slot, slot_next):
        def body(tt, c):
            for dt in range(TOKENS_PER_ITER):
                wait(slot, tt * TOKENS_PER_ITER + dt)
            for dt in range(TOKENS_PER_ITER):
                t = tt * TOKENS_PER_ITER + dt
                mix_token(grp * GATHER_GROUP + t, t, slot, (grp + 1) * group_picks, slot_next)
            return c
        lax.fori_loop(0, GATHER_GROUP // TOKENS_PER_ITER, body, 0)

    @pl.when(i == 0)
    def _():
        def body(k, c):
            start(k, 0, k // PEER_SLOTS, k)
            return c
        lax.fori_loop(0, group_picks, body, 0, unroll=8)

    def pair(gg, c):
        mix_group(2 * gg, 0, 1)
        mix_group(2 * gg + 1, 1, 0)
        return c
    lax.fori_loop(0, n_groups // 2, pair, 0)

    @pl.when(i == pl.num_programs(0) - 1)
    def _():
        for t in range(GATHER_GROUP):
            wait(0, t)

    y = o_ref[...]
    yc = y - _tile_mean(y)
    o_ref[...] = yc * lax.rsqrt(_tile_mean(yc * yc) + LN_EPS) * lg_ref[...] + lb_ref[...]


def _pack_kernel(lo_ref, hi_ref, o_ref):
    bits = lambda r: lax.bitcast_convert_type(r[...].astype(jnp.bfloat16).astype(jnp.float32), jnp.uint32)
    packed = (bits(lo_ref) >> 16) | (bits(hi_ref) & jnp.uint32(0xFFFF0000))
    for c in range(o_ref.shape[1]):
        o_ref[:, c, :] = packed[:, c * LANES:(c + 1) * LANES]


def _pack_bf16_pair(lo, hi):
    n, d = lo.shape
    tm = 512
    return pl.pallas_call(
        _pack_kernel,
        grid=(n // tm,),
        in_specs=[pl.BlockSpec((tm, d), lambda i: (i, 0)), pl.BlockSpec((tm, d), lambda i: (i, 0))],
        out_specs=pl.BlockSpec((tm, d // LANES, LANES), lambda i: (i, 0, 0)),
        out_shape=jax.ShapeDtypeStruct((n, d // LANES, LANES), jnp.uint32),
        compiler_params=_params("arbitrary"),
        name="pack_tables",
    )(lo, hi)


def _peer_gather_ln(x, experts, gates, uv, ln_g, ln_b, alpha):
    n, d = x.shape
    tb = _row_block(n, GATHER_BLOCK)
    nblk = n // tb
    assert (tb // GATHER_GROUP) % 2 == 0 and d == SUBLANES * LANES
    group_picks = GATHER_GROUP * PEER_SLOTS
    picks = experts.T.reshape(nblk, tb * PEER_SLOTS)
    picks = jnp.concatenate([picks, jnp.roll(picks[:, :group_picks], -1, axis=0)], axis=1).reshape(-1)
    buf = pltpu.VMEM((group_picks, SUBLANES, LANES), jnp.uint32)
    tile = (SUBLANES, LANES)
    out = pl.pallas_call(
        functools.partial(_peer_gather_kernel, tb=tb, d=d, alpha=alpha),
        grid=(nblk,),
        in_specs=[pl.BlockSpec((tb * PEER_SLOTS + group_picks,), lambda i: (i,), memory_space=pltpu.SMEM),
                  pl.BlockSpec((tb,) + tile, lambda i: (i, 0, 0)),
                  pl.BlockSpec((1, PEER_SLOTS, tb), lambda i: (i, 0, 0)),
                  pl.BlockSpec(tile, lambda i: (0, 0)), pl.BlockSpec(tile, lambda i: (0, 0)),
                  pl.BlockSpec(memory_space=pl.ANY)],
        out_specs=pl.BlockSpec((tb,) + tile, lambda i: (i, 0, 0)),
        out_shape=jax.ShapeDtypeStruct((n,) + tile, jnp.float32),
        scratch_shapes=[buf, buf, pltpu.SemaphoreType.DMA((2, GATHER_GROUP))],
        compiler_params=_params("arbitrary", disable_bounds_checks=True),
        name="peer_gather",
    )(picks, x.reshape((n,) + tile), jnp.swapaxes(gates.reshape(PEER_SLOTS, nblk, tb), 0, 1),
      ln_g.reshape(tile), ln_b.reshape(tile), uv)
    return out.reshape(n, d)


def kernel(x_prompt, x_sample, cache_k, cache_v, cache_logf, state_conv, page_table, w_in_ab, b_f, conv_w, w_o_ab,
           w_in_c, sg_ln_g, sg_ln_b, w_s, b_s, w_o_c, ln1_g, ln1_b, ln2_g, ln2_b, peer_wq, peer_keys, peer_u, peer_v):
    n_seq, t, d = x_prompt.shape
    n_dec = x_sample.shape[0]
    depth = ln1_g.shape[0]
    d_conv = conv_w.shape[2]
    d_att = N_HEADS * HEAD_DIM
    d_in = w_in_ab.shape[2]
    d_in_pad = -(-d_in // (5 * LANES)) * (5 * LANES)
    alpha = (2.0 * depth) ** 0.25
    bf16 = jnp.bfloat16

    xp = x_prompt.reshape(n_seq * t, d)
    xs = x_sample.reshape(n_dec, d)
    n_phys = cache_k.shape[0]
    ck = cache_k.reshape(n_phys, -1, PAGE, d_att)
    cv = cache_v.reshape(n_phys, -1, PAGE, d_att)
    clf_t = jnp.swapaxes(cache_logf, 2, 3)

    kp_l, vp_l, fp_l, cp_l, ks_l, vs_l, fs_l, cs_l, sv_l = [], [], [], [], [], [], [], [], []
    for l in range(depth):
        i = l // 2
        if l % 2 == 0:
            w_in = jnp.pad(w_in_ab[i], ((0, 0), (0, d_in_pad - d_in))).astype(bf16)
            w_o = w_o_ab[i].astype(bf16)
            xw = _matmul(xp, w_in, d_in_pad // 5)
            a_out, lf, conv_new = _conv_prompt(xw, b_f[i], conv_w[i], n_seq, d_conv)
            cum_t = _cumsum_time(jnp.swapaxes(lf.reshape(n_seq, t, N_HEADS), 1, 2))
            cum = jnp.swapaxes(cum_t, 1, 2).reshape(n_seq * t, N_HEADS)
            o = _fox_prompt(xw, cum, cum_t, n_seq, d_conv)
            kp_l.append(xw[:, 3 * d_conv + d_att:3 * d_conv + 2 * d_att].reshape(n_seq, t, N_HEADS, HEAD_DIM))
            vp_l.append(xw[:, 3 * d_conv + 2 * d_att:3 * d_conv + 3 * d_att].reshape(n_seq, t, N_HEADS, HEAD_DIM))
            fp_l.append(lf.reshape(n_seq, t, N_HEADS))
            cp_l.append(conv_new)
            xp = _matmul_res_ln([a_out, o], [w_o[:d_conv], w_o[d_conv:]], xp, ln1_g[l], ln1_b[l], alpha)
            xw = _matmul(xs, w_in, d_in_pad // 5)
            a_out, lf, z = _conv_sample(xw, state_conv[:, i, 0], state_conv[:, i, 1], b_f[i], conv_w[i], d_conv)
            q = xw[:, 3 * d_conv:3 * d_conv + d_att]
            k = xw[:, 3 * d_conv + d_att:3 * d_conv + 2 * d_att]
            v = xw[:, 3 * d_conv + 2 * d_att:3 * d_conv + 3 * d_att]
            o = _fox_sample(q, k, v, lf, ck, cv, clf_t, page_table, i)
            ks_l.append(k.reshape(n_dec, 1, N_HEADS, HEAD_DIM))
            vs_l.append(v.reshape(n_dec, 1, N_HEADS, HEAD_DIM))
            fs_l.append(lf.reshape(n_dec, 1, N_HEADS))
            cs_l.append(jnp.stack([state_conv[:, i, 1], z], axis=1))
            xs = _matmul_res_ln([a_out, o], [w_o[:d_conv], w_o[d_conv:]], xs, ln1_g[l], ln1_b[l], alpha)
        else:
            w_in = w_in_c[i].astype(bf16)
            w_o = w_o_c[i].astype(bf16)
            d_sg = w_o.shape[0]
            hw = _matmul(xp, w_in, d_sg)
            sg = _sg_prompt(hw, sg_ln_g[i], sg_ln_b[i], w_s[i], b_s[i].T)
            xp = _matmul_res_ln([sg], [w_o], xp, ln1_g[l], ln1_b[l], alpha)
            hw = _matmul(xs, w_in, d_sg)
            gd = d_sg // SG_GROUPS
            sg, sv = _sg_sample(hw, sg_ln_g[i], sg_ln_b[i], jnp.repeat(w_s[i, :, 0, 0], gd), jnp.repeat(b_s[i, :, 0], gd))
            sv_l.append(sv.reshape(n_dec, 1, d_sg))
            xs = _matmul_res_ln([sg], [w_o], xs, ln1_g[l], ln1_b[l], alpha)
        wq = peer_wq[l].astype(bf16)
        uv = _pack_bf16_pair(peer_u[l], peer_v[l])
        for which in range(2):
            xx = xp if which == 0 else xs
            qq = _matmul(xx, wq, wq.shape[1] // 2)
            experts, gates = _peer_topk(qq, peer_keys[l])
            xx = _peer_gather_ln(xx, experts, gates, uv, ln2_g[l], ln2_b[l], alpha)
            if which == 0:
                xp = xx
            else:
                xs = xx

    return (xp.reshape(n_seq, t, d), xs.reshape(n_dec, 1, d),
            jnp.stack(kp_l, axis=1), jnp.stack(vp_l, axis=1), jnp.stack(fp_l, axis=1),
            jnp.stack(ks_l, axis=1), jnp.stack(vs_l, axis=1), jnp.stack(fs_l, axis=1),
            jnp.stack(cp_l, axis=1), jnp.stack(cs_l, axis=1), jnp.stack(sv_l, axis=1))
```
